```python
import math
import jax, jax.numpy as jnp
from jax import lax
import numpy as np

D_MODEL = 4096
BATCH = 8
SEQ = 2048
DEPTH = 2

N_A_LAYERS = (DEPTH + 1) // 2
N_B_LAYERS = DEPTH // 2

CONV_WIDTH = 3

N_HEADS = 32
HEAD_DIM = D_MODEL // N_HEADS
N_KV_GROUPS = 4
HEADS_PER_GROUP = N_HEADS // N_KV_GROUPS
KV_WIDTH = N_KV_GROUPS * HEAD_DIM
CMP_BLOCK = 32
CMP_STRIDE = 16
CMP_HIDDEN = 256
SEL_BLOCK = 64
SEL_TOP_N = 16
SEL_Q_CHUNK = 16
WINDOW = 512
WIN_Q_BLOCK = 128
ROPE_THETA = 10000.0
NSA_IN_WIDTH = N_HEADS * HEAD_DIM + 6 * KV_WIDTH + 3 * N_HEADS

D_FF = 11008
N_EXPERTS = 8
TOP_K = 2
D_FF_EXPERT = 3584

EPS = 1e-6
NEG_INF = -1e30
FORCE_BONUS = 1e4

kernel_name = "hybrid_shortconv_nsa_moe_adaln"


def rms_norm(x, g):
    xf = x.astype(jnp.float32)
    y = xf * lax.rsqrt(jnp.mean(xf * xf, axis=-1, keepdims=True) + EPS)
    return (y * g.astype(jnp.float32)).astype(x.dtype)


def rope(x, pos):
    half = HEAD_DIM // 2
    inv = ROPE_THETA ** (-jnp.arange(half, dtype=jnp.float32) / half)
    ang = pos.astype(jnp.float32)[:, None] * inv[None, :]
    cos = jnp.cos(ang)[:, None, :]
    sin = jnp.sin(ang)[:, None, :]
    xf = x.astype(jnp.float32)
    x1, x2 = xf[..., :half], xf[..., half:]
    return jnp.concatenate([x1 * cos - x2 * sin, x2 * cos + x1 * sin], axis=-1).astype(x.dtype)


def swiglu(t, w_gate, w_up, w_down):
    return (jax.nn.silu(t @ w_gate) * (t @ w_up)) @ w_down


def short_conv_mixer(h, w_in, conv_w, w_out):
    gate_b, gate_c, u = jnp.split(h @ w_in, 3, axis=-1)
    v = gate_c * u
    y = lax.conv_general_dilated(
        v, conv_w[:, None, :].astype(v.dtype), window_strides=(1,),
        padding=[(CONV_WIDTH - 1, 0)], dimension_numbers=('NWC', 'WIO', 'NWC'),
        feature_group_count=D_MODEL)
    return (gate_b * y) @ w_out


def nsa_mixer(h, w_in, w_o, q_norm_g, k_norm_g, cmp_pos, cmp_w1, cmp_w2):
    B_, S_, _ = h.shape
    H, G, J, Dh = N_HEADS, N_KV_GROUPS, HEADS_PER_GROUP, HEAD_DIM
    f32 = jnp.float32
    scale = Dh ** -0.5
    pos = jnp.arange(S_, dtype=jnp.int32)

    proj = h @ w_in
    qd = H * Dh
    q = proj[..., :qd].reshape(B_, S_, H, Dh)
    kv = proj[..., qd:qd + 6 * KV_WIDTH].reshape(B_, S_, 6, G, Dh)
    gates = jax.nn.sigmoid(proj[..., qd + 6 * KV_WIDTH:].astype(f32)).reshape(B_, S_, H, 3)
    k_c, v_c, k_s, v_s, k_w, v_w = [kv[:, :, i] for i in range(6)]

    q = rope(rms_norm(q, q_norm_g), pos)
    qg = q.reshape(B_, S_, G, J, Dh)

    n_cmp = S_ // CMP_STRIDE - 1
    cmp_end = jnp.arange(n_cmp, dtype=jnp.int32) * CMP_STRIDE + CMP_BLOCK - 1

    def compress(t, p, w1, w2):
        ch = t.reshape(B_, S_ // CMP_STRIDE, CMP_STRIDE, G, Dh)
        blocks = jnp.concatenate([ch[:, :-1], ch[:, 1:]], axis=2) + p[None, None, :, None, :]
        hid = jax.nn.silu(jnp.einsum('bnlgd,ldf->bngf', blocks, w1))
        return hid @ w2

    k_cmp = rope(rms_norm(compress(k_c, cmp_pos[0], cmp_w1[0], cmp_w2[0]), k_norm_g[0]), cmp_end)
    v_cmp = compress(v_c, cmp_pos[1], cmp_w1[1], cmp_w2[1]).astype(f32)
    s_c = jnp.einsum('bsgjd,bngd->bgjsn', qg, k_cmp).astype(f32) * scale
    mask_c = cmp_end[None, :] <= pos[:, None]
    p_c = jax.nn.softmax(jnp.where(mask_c, s_c, NEG_INF), axis=-1) * mask_c
    o_c = jnp.einsum('bgjsn,bngd->bsgjd', p_c, v_cmp)

    n_sb = S_ // SEL_BLOCK
    ci = jnp.arange(n_cmp, dtype=jnp.int32)[:, None] * CMP_STRIDE
    sj = jnp.arange(n_sb, dtype=jnp.int32)[None, :] * SEL_BLOCK
    overlap = jnp.clip(jnp.minimum(ci + CMP_BLOCK, sj + SEL_BLOCK) - jnp.maximum(ci, sj), 0)
    overlap = overlap.astype(f32) / CMP_BLOCK
    imp = jnp.einsum('bgjsn,nm->bgsm', p_c, overlap)
    blk = jnp.arange(n_sb, dtype=jnp.int32)[None, :]
    cur = (pos // SEL_BLOCK)[:, None]
    forced = (blk == 0) | (blk == cur) | (blk == cur - 1)
    causal_blk = blk * SEL_BLOCK <= pos[:, None]
    imp = jnp.where(causal_blk, imp + FORCE_BONUS * forced.astype(f32), NEG_INF)
    n_top = min(SEL_TOP_N, n_sb)
    _, sel_idx = lax.top_k(imp, n_top)

    ks_b = rope(rms_norm(k_s, k_norm_g[1]), pos).reshape(B_, n_sb, SEL_BLOCK, G, Dh).transpose(0, 3, 1, 2, 4)
    vs_b = v_s.reshape(B_, n_sb, SEL_BLOCK, G, Dh).transpose(0, 3, 1, 2, 4)
    n_qc = S_ // SEL_Q_CHUNK
    q_ch = qg.reshape(B_, n_qc, SEL_Q_CHUNK, G, J, Dh).transpose(1, 0, 2, 3, 4, 5)
    idx_ch = sel_idx.reshape(B_, G, n_qc, SEL_Q_CHUNK, n_top).transpose(2, 0, 1, 3, 4)
    pos_ch = pos.reshape(n_qc, SEL_Q_CHUNK)
    bi = jnp.arange(B_)[:, None, None, None]
    gi = jnp.arange(G)[None, :, None, None]
    offs = jnp.arange(SEL_BLOCK, dtype=jnp.int32)

    def sel_chunk(args):
        qc, ic, tc = args
        kg = ks_b[bi, gi, ic]
        vg = vs_b[bi, gi, ic].astype(f32)
        s = jnp.einsum('bqgjd,bgqnkd->bgjqnk', qc, kg).astype(f32) * scale
        kpos = ic[..., None] * SEL_BLOCK + offs
        m = (kpos <= tc[None, None, :, None, None])[:, :, None]
        s = jnp.where(m, s, NEG_INF).reshape(B_, G, J, SEL_Q_CHUNK, n_top * SEL_BLOCK)
        p = jax.nn.softmax(s, axis=-1).reshape(B_, G, J, SEL_Q_CHUNK, n_top, SEL_BLOCK)
        return jnp.einsum('bgjqnk,bgqnkd->bqgjd', p, vg)

    o_s = lax.map(sel_chunk, (q_ch, idx_ch, pos_ch))
    o_s = o_s.transpose(1, 0, 2, 3, 4, 5).reshape(B_, S_, G, J, Dh)

    kw = rope(rms_norm(k_w, k_norm_g[2]), pos)
    kp = jnp.pad(kw, ((0, 0), (WINDOW, 0), (0, 0), (0, 0)))
    vp = jnp.pad(v_w, ((0, 0), (WINDOW, 0), (0, 0), (0, 0)))
    n_wb = S_ // WIN_Q_BLOCK
    span = WIN_Q_BLOCK + WINDOW
    q_wb = qg.reshape(B_, n_wb, WIN_Q_BLOCK, G, J, Dh).transpose(1, 0, 2, 3, 4, 5)

    def win_block(args):
        b_idx, qb = args
        start = b_idx * WIN_Q_BLOCK
        kb = lax.dynamic_slice_in_dim(kp, start, span, axis=1)
        vb = lax.dynamic_slice_in_dim(vp, start, span, axis=1).astype(f32)
        tq = start + jnp.arange(WIN_Q_BLOCK, dtype=jnp.int32)
        tk = start - WINDOW + jnp.arange(span, dtype=jnp.int32)
        diff = tq[:, None] - tk[None, :]
        m = (diff >= 0) & (diff < WINDOW) & (tk[None, :] >= 0)
        s = jnp.einsum('bqgjd,bkgd->bgjqk', qb, kb).astype(f32) * scale
        p = jax.nn.softmax(jnp.where(m, s, NEG_INF), axis=-1)
        return jnp.einsum('bgjqk,bkgd->bqgjd', p, vb)

    o_w = lax.map(win_block, (jnp.arange(n_wb, dtype=jnp.int32), q_wb))
    o_w = o_w.transpose(1, 0, 2, 3, 4, 5).reshape(B_, S_, G, J, Dh)

    o = (gates[..., 0:1] * o_c.reshape(B_, S_, H, Dh)
         + gates[..., 1:2] * o_s.reshape(B_, S_, H, Dh)
         + gates[..., 2:3] * o_w.reshape(B_, S_, H, Dh))
    return o.reshape(B_, S_, H * Dh).astype(h.dtype) @ w_o


def moe_ffn(h, router_w, router_b, w_gate, w_up, w_down):
    B_, S_, D = h.shape
    t = h.reshape(B_ * S_, D)
    logits = (t @ router_w).astype(jnp.float32) + router_b.astype(jnp.float32)
    top_v, top_i = lax.top_k(logits, TOP_K)
    top_w = jax.nn.softmax(top_v, axis=-1)
    gate = jnp.sum(jax.nn.one_hot(top_i, N_EXPERTS, dtype=jnp.float32) * top_w[..., None], axis=1)
    out = jnp.zeros((B_ * S_, D), jnp.float32)
    for e in range(N_EXPERTS):
        out = out + gate[:, e:e + 1] * swiglu(t, w_gate[e], w_up[e], w_down[e])
    return out.astype(h.dtype).reshape(B_, S_, D)


def _normal(key, shape, scale):
    return jax.random.normal(key, shape, jnp.float32) * scale


def setup_inputs(seed: int = 0) -> dict:
    key = jax.random.key(seed)
    ks = jax.random.split(key, 26)
    D = D_MODEL
    NA, NB = N_A_LAYERS, N_B_LAYERS
    return {
        "x": _normal(ks[0], (BATCH, SEQ, D), 1.0),
        "c": _normal(ks[1], (BATCH, D), 1.0),
        "ada_w": _normal(ks[2], (DEPTH, D, 6 * D), D ** -0.5),
        "ada_b": _normal(ks[3], (DEPTH, 6 * D), 0.02),
        "norm_mix_g": 1.0 + _normal(ks[4], (DEPTH, D), 0.02),
        "norm_ffn_g": 1.0 + _normal(ks[5], (DEPTH, D), 0.02),
        "conv_w_in": _normal(ks[6], (NA, D, 3 * D), D ** -0.5),
        "conv_w": _normal(ks[7], (NA, CONV_WIDTH, D), CONV_WIDTH ** -0.5),
        "conv_w_out": _normal(ks[8], (NA, D, D), D ** -0.5),
        "nsa_w_in": _normal(ks[9], (NB, D, NSA_IN_WIDTH), D ** -0.5),
        "nsa_w_o": _normal(ks[10], (NB, N_HEADS * HEAD_DIM, D), (N_HEADS * HEAD_DIM) ** -0.5),
        "nsa_q_norm_g": 1.0 + _normal(ks[11], (NB, HEAD_DIM), 0.02),
        "nsa_k_norm_g": 1.0 + _normal(ks[12], (NB, 3, HEAD_DIM), 0.02),
        "cmp_pos": _normal(ks[13], (NB, 2, CMP_BLOCK, HEAD_DIM), 0.5),
        "cmp_w1": _normal(ks[14], (NB, 2, CMP_BLOCK, HEAD_DIM, CMP_HIDDEN), (CMP_BLOCK * HEAD_DIM) ** -0.5),
        "cmp_w2": _normal(ks[15], (NB, 2, CMP_HIDDEN, HEAD_DIM), CMP_HIDDEN ** -0.5),
        "ffn_w_gate": _normal(ks[16], (NA, D, D_FF), D ** -0.5),
        "ffn_w_up": _normal(ks[17], (NA, D, D_FF), D ** -0.5),
        "ffn_w_down": _normal(ks[18], (NA, D_FF, D), D_FF ** -0.5),
        "router_w": _normal(ks[19], (NB, D, N_EXPERTS), D ** -0.5),
        "router_b": _normal(ks[20], (NB, N_EXPERTS), 0.01),
        "moe_w_gate": _normal(ks[21], (NB, N_EXPERTS, D, D_FF_EXPERT), D ** -0.5),
        "moe_w_up": _normal(ks[22], (NB, N_EXPERTS, D, D_FF_EXPERT), D ** -0.5),
        "moe_w_down": _normal(ks[23], (NB, N_EXPERTS, D_FF_EXPERT, D), D_FF_EXPERT ** -0.5),
    }


def reference(x, c, ada_w, ada_b, norm_mix_g, norm_ffn_g, conv_w_in, conv_w, conv_w_out,
              nsa_w_in, nsa_w_o, nsa_q_norm_g, nsa_k_norm_g, cmp_pos, cmp_w1, cmp_w2,
              ffn_w_gate, ffn_w_up, ffn_w_down, router_w, router_b,
              moe_w_gate, moe_w_up, moe_w_down):
    cond = jax.nn.silu(c)
    for i in range(DEPTH):
        j = i // 2
        mod = (cond @ ada_w[i] + ada_b[i])[:, None, :]
        sh_m, sc_m, g_m, sh_f, sc_f, g_f = jnp.split(mod, 6, axis=-1)
        hm = rms_norm(x, norm_mix_g[i]) * (1 + sc_m) + sh_m
        if i % 2 == 0:
            y = short_conv_mixer(hm, conv_w_in[j], conv_w[j], conv_w_out[j])
        else:
            y = nsa_mixer(hm, nsa_w_in[j], nsa_w_o[j], nsa_q_norm_g[j], nsa_k_norm_g[j],
                          cmp_pos[j], cmp_w1[j], cmp_w2[j])
        x = x + g_m * y
        hf = rms_norm(x, norm_ffn_g[i]) * (1 + sc_f) + sh_f
        if i % 2 == 0:
            f = swiglu(hf, ffn_w_gate[j], ffn_w_up[j], ffn_w_down[j])
        else:
            f = moe_ffn(hf, router_w[j], router_b[j], moe_w_gate[j], moe_w_up[j], moe_w_down[j])
        x = x + g_f * f
    return x
```

```python
import functools

import jax
import jax.numpy as jnp
from jax import lax
from jax.experimental import pallas as pl
from jax.experimental.pallas import tpu as pltpu

F32 = jnp.float32
BF16 = jnp.bfloat16
I32 = jnp.int32

EPS = 1e-6
NEG_INF = -1e30
FORCE_BONUS = 1e4
ROPE_THETA = 10000.0

CONV_WIDTH = 3
N_HEADS = 32
HEAD_DIM = 128
N_KV_GROUPS = 4
HEADS_PER_GROUP = N_HEADS // N_KV_GROUPS
KV_WIDTH = N_KV_GROUPS * HEAD_DIM
CMP_BLOCK = 32
CMP_STRIDE = 16
SEL_BLOCK = 64
SEL_TOP_N = 16
WINDOW = 512
N_EXPERTS = 8

LANES = 128
VMEM_LIMIT_V7X = 56 * 1024 * 1024

TM = 1024
TN = 1024
TN_RES = 512
TN_GLU = 256
TK_DOWN = 5504
TM_ROW = 512
TC_CONV = 512
TQ = 128
TK_SEL = 512
TM_MOE = 512
TN_MOE = 512
TG = 256
TCMB = 128


def _params(*sem):
    return pltpu.CompilerParams(dimension_semantics=sem, vmem_limit_bytes=VMEM_LIMIT_V7X)


def _tile(full, want):
    return want if full % want == 0 else full


def _ada_kernel(c_ref, w_ref, b_ref, o_ref):
    c = c_ref[...]
    cond = c * jax.nn.sigmoid(c)
    acc = jnp.dot(cond.astype(BF16), w_ref[...].astype(BF16), preferred_element_type=F32)
    o_ref[...] = acc + b_ref[...]


def _ada_mod(c, ada_w, ada_b):
    depth, d, n = ada_w.shape
    b = c.shape[0]
    tn = _tile(n, 512)
    return pl.pallas_call(
        _ada_kernel,
        out_shape=jax.ShapeDtypeStruct((depth, b, n), F32),
        grid=(depth, n // tn),
        in_specs=[
            pl.BlockSpec((b, d), lambda l, j: (0, 0)),
            pl.BlockSpec((None, d, tn), lambda l, j: (l, 0, j)),
            pl.BlockSpec((None, 1, tn), lambda l, j: (l, 0, j)),
        ],
        out_specs=pl.BlockSpec((None, b, tn), lambda l, j: (l, 0, j)),
        compiler_params=_params("parallel", "parallel"),
        name="ada_mod",
    )(c, ada_w, ada_b.reshape(depth, 1, n))


def _norm_mod_value(x_ref, g_ref, sc_ref, sh_ref):
    x = x_ref[...]
    ms = jnp.mean(x * x, axis=-1, keepdims=True)
    y = x * lax.rsqrt(ms + EPS) * g_ref[...]
    return y * (1.0 + sc_ref[...]) + sh_ref[...]


def _norm_mod_kernel(x_ref, g_ref, sc_ref, sh_ref, o_ref):
    o_ref[...] = _norm_mod_value(x_ref, g_ref, sc_ref, sh_ref).astype(o_ref.dtype)


def _norm_mod_router_kernel(x_ref, g_ref, sc_ref, sh_ref, rw_ref, rb_ref, o_ref, lg_ref):
    h = _norm_mod_value(x_ref, g_ref, sc_ref, sh_ref)
    o_ref[...] = h.astype(o_ref.dtype)
    lg_ref[...] = jnp.dot(h, rw_ref[...], preferred_element_type=F32,
                          precision=lax.Precision.HIGHEST) + rb_ref[...]


def _norm_mod(x, g, sc, sh, seq, router=None):
    t, d = x.shape
    b = sc.shape[0]
    tm = _tile(seq, TM_ROW)
    row_specs = [
        pl.BlockSpec((tm, d), lambda i: (i, 0)),
        pl.BlockSpec((1, d), lambda i: (0, 0)),
        pl.BlockSpec((None, 1, d), lambda i: (i * tm // seq, 0, 0)),
        pl.BlockSpec((None, 1, d), lambda i: (i * tm // seq, 0, 0)),
    ]
    args = [x, g.reshape(1, d), sc.reshape(b, 1, d), sh.reshape(b, 1, d)]
    if router is None:
        return pl.pallas_call(
            _norm_mod_kernel,
            out_shape=jax.ShapeDtypeStruct((t, d), BF16),
            grid=(t // tm,),
            in_specs=row_specs,
            out_specs=pl.BlockSpec((tm, d), lambda i: (i, 0)),
            compiler_params=_params("parallel"),
            name="norm_mod",
        )(*args)
    rw, rb = router
    return pl.pallas_call(
        _norm_mod_router_kernel,
        out_shape=(jax.ShapeDtypeStruct((t, d), BF16), jax.ShapeDtypeStruct((t, LANES), F32)),
        grid=(t // tm,),
        in_specs=row_specs + [
            pl.BlockSpec((d, LANES), lambda i: (0, 0)),
            pl.BlockSpec((1, LANES), lambda i: (0, 0)),
        ],
        out_specs=(pl.BlockSpec((tm, d), lambda i: (i, 0)), pl.BlockSpec((tm, LANES), lambda i: (i, 0))),
        compiler_params=_params("parallel"),
        name="norm_mod_router",
    )(*args, rw, rb)


def _mm_kernel(x_ref, w_ref, o_ref):
    o_ref[...] = jnp.dot(x_ref[...], w_ref[...], preferred_element_type=F32).astype(o_ref.dtype)


def _mm(x, w, out_dtype, name, tm=TM, tn=TN):
    m, k = x.shape
    n = w.shape[1]
    tm, tn = _tile(m, tm), _tile(n, tn)
    return pl.pallas_call(
        _mm_kernel,
        out_shape=jax.ShapeDtypeStruct((m, n), out_dtype),
        grid=(m // tm, n // tn),
        in_specs=[pl.BlockSpec((tm, k), lambda i, j: (i, 0)), pl.BlockSpec((k, tn), lambda i, j: (0, j))],
        out_specs=pl.BlockSpec((tm, tn), lambda i, j: (i, j)),
        compiler_params=_params("parallel", "parallel"),
        name=name,
    )(x, w)


def _mm_res_kernel(x_ref, w_ref, res_ref, g_ref, o_ref, *acc, nk):
    part = jnp.dot(x_ref[...], w_ref[...], preferred_element_type=F32)
    if nk == 1:
        o_ref[...] = res_ref[...] + g_ref[...] * part
        return
    acc_ref, = acc
    k = pl.program_id(2)

    @pl.when(k == 0)
    def _():
        acc_ref[...] = part

    @pl.when(k > 0)
    def _():
        acc_ref[...] += part

    @pl.when(k == nk - 1)
    def _():
        o_ref[...] = res_ref[...] + g_ref[...] * acc_ref[...]


def _mm_res(x, w, res, gate, seq, name, tm=TM, tn=TN_RES, tk=None):
    m, k = x.shape
    n = w.shape[1]
    b = gate.shape[0]
    tm, tn = _tile(seq, tm), _tile(n, tn)
    tk = k if tk is None else _tile(k, tk)
    nk = k // tk
    scratch = [] if nk == 1 else [pltpu.VMEM((tm, tn), F32)]
    return pl.pallas_call(
        functools.partial(_mm_res_kernel, nk=nk),
        out_shape=jax.ShapeDtypeStruct((m, n), F32),
        grid=(m // tm, n // tn, nk),
        in_specs=[
            pl.BlockSpec((tm, tk), lambda i, j, kk: (i, kk)),
            pl.BlockSpec((tk, tn), lambda i, j, kk: (kk, j)),
            pl.BlockSpec((tm, tn), lambda i, j, kk: (i, j)),
            pl.BlockSpec((None, 1, tn), lambda i, j, kk: (i * tm // seq, 0, j)),
        ],
        out_specs=pl.BlockSpec((tm, tn), lambda i, j, kk: (i, j)),
        scratch_shapes=scratch,
        compiler_params=_params("parallel", "parallel", "arbitrary"),
        name=name,
    )(x, w, res, gate.reshape(b, 1, n))


def _silu_mul(a, b):
    return (a * jax.nn.sigmoid(a)) * b


def _glu_kernel(x_ref, wg_ref, wu_ref, o_ref):
    x = x_ref[...]
    a = jnp.dot(x, wg_ref[...], preferred_element_type=F32)
    b = jnp.dot(x, wu_ref[...], preferred_element_type=F32)
    o_ref[...] = _silu_mul(a, b).astype(o_ref.dtype)


def _glu(x, wg, wu, name, tm=TM, tn=TN_GLU):
    m, k = x.shape
    n = wg.shape[1]
    tm, tn = _tile(m, tm), _tile(n, tn)
    return pl.pallas_call(
        _glu_kernel,
        out_shape=jax.ShapeDtypeStruct((m, n), BF16),
        grid=(m // tm, n // tn),
        in_specs=[
            pl.BlockSpec((tm, k), lambda i, j: (i, 0)),
            pl.BlockSpec((k, tn), lambda i, j: (0, j)),
            pl.BlockSpec((k, tn), lambda i, j: (0, j)),
        ],
        out_specs=pl.BlockSpec((tm, tn), lambda i, j: (i, j)),
        compiler_params=_params("parallel", "parallel"),
        name=name,
    )(x, wg, wu)


def _conv_gate_kernel(b_ref, c_ref, u_ref, w_ref, o_ref):
    v = c_ref[...].astype(F32) * u_ref[...].astype(F32)
    rows = lax.broadcasted_iota(I32, v.shape, 0)
    v1 = jnp.where(rows >= 1, pltpu.roll(v, 1, 0), 0.0)
    v2 = jnp.where(rows >= 2, pltpu.roll(v, 2, 0), 0.0)
    w = w_ref[...]
    y = w[0:1, :] * v2 + w[1:2, :] * v1 + w[2:3, :] * v
    o_ref[...] = (b_ref[...].astype(F32) * y).astype(o_ref.dtype)


def _conv_gate(bcu, conv_w, batch, seq):
    d = conv_w.shape[1]
    tc = _tile(d, TC_CONV)
    nc = d // tc
    bcu3 = bcu.reshape(batch, seq, 3 * d)
    out = pl.pallas_call(
        _conv_gate_kernel,
        out_shape=jax.ShapeDtypeStruct((batch, seq, d), BF16),
        grid=(batch, nc),
        in_specs=[
            pl.BlockSpec((None, seq, tc), lambda b, j: (b, 0, j)),
            pl.BlockSpec((None, seq, tc), lambda b, j: (b, 0, nc + j)),
            pl.BlockSpec((None, seq, tc), lambda b, j: (b, 0, 2 * nc + j)),
            pl.BlockSpec((CONV_WIDTH, tc), lambda b, j: (0, j)),
        ],
        out_specs=pl.BlockSpec((None, seq, tc), lambda b, j: (b, 0, j)),
        compiler_params=_params("parallel", "parallel"),
        name="conv_gate",
    )(bcu3, bcu3, bcu3, conv_w)
    return out.reshape(batch * seq, d)


def _rope_tables(pos):
    half = HEAD_DIM // 2
    inv = ROPE_THETA ** (-jnp.arange(half, dtype=F32) / half)
    ang = pos.astype(F32)[:, None] * inv[None, :]
    cos, sin = jnp.cos(ang), jnp.sin(ang)
    return jnp.concatenate([cos, cos], axis=-1), jnp.concatenate([-sin, sin], axis=-1)


def _norm_rope(x, g, cos, sin_signed):
    ms = jnp.mean(x * x, axis=-1, keepdims=True)
    xn = x * lax.rsqrt(ms + EPS) * g
    return xn * cos + pltpu.roll(xn, HEAD_DIM // 2, 1) * sin_signed


def _nsa_prep_kernel(p_ref, gp_ref, cos_ref, sin_ref, qg_ref, kg_ref,
                     q_o, ks_o, vs_o, kw_o, vw_o, g_o, *, scale):
    cos, sin = cos_ref[...], sin_ref[...]
    hd = HEAD_DIM
    qg = qg_ref[...]
    for h in range(N_HEADS):
        x = p_ref[:, h * hd:(h + 1) * hd]
        q_o[:, h * hd:(h + 1) * hd] = (_norm_rope(x, qg, cos, sin) * scale).astype(q_o.dtype)
    kv0 = N_HEADS * hd
    for g in range(N_KV_GROUPS):
        def col(i, g=g):
            c0 = kv0 + i * KV_WIDTH + g * hd
            return p_ref[:, c0:c0 + hd]
        sl = slice(g * hd, (g + 1) * hd)
        ks_o[:, sl] = _norm_rope(col(2), kg_ref[1:2, :], cos, sin).astype(ks_o.dtype)
        vs_o[:, sl] = col(3).astype(vs_o.dtype)
        kw_o[:, sl] = _norm_rope(col(4), kg_ref[2:3, :], cos, sin).astype(kw_o.dtype)
        vw_o[:, sl] = col(5).astype(vw_o.dtype)
    sig = jax.nn.sigmoid(gp_ref[...])
    per_group = 3 * HEADS_PER_GROUP
    for g in range(N_KV_GROUPS):
        shift = (LANES - g * per_group) % LANES
        g_o[:, g * LANES:(g + 1) * LANES] = sig if shift == 0 else pltpu.roll(sig, shift, 1)


def _nsa_prep(proj, gates_pre, cos, sin, q_g, k_g, seq):
    t = proj.shape[0]
    tm = _tile(seq, 256)
    nseq = seq // tm
    qd = N_HEADS * HEAD_DIM
    kvspec = pl.BlockSpec((tm, KV_WIDTH), lambda i: (i, 0))
    kvshape = jax.ShapeDtypeStruct((t, KV_WIDTH), BF16)
    return pl.pallas_call(
        functools.partial(_nsa_prep_kernel, scale=HEAD_DIM ** -0.5),
        out_shape=(jax.ShapeDtypeStruct((t, qd), BF16), kvshape, kvshape, kvshape, kvshape,
                   jax.ShapeDtypeStruct((t, N_KV_GROUPS * LANES), F32)),
        grid=(t // tm,),
        in_specs=[
            pl.BlockSpec((tm, proj.shape[1]), lambda i: (i, 0)),
            pl.BlockSpec((tm, LANES), lambda i: (i, 0)),
            pl.BlockSpec((tm, HEAD_DIM), lambda i: (i % nseq, 0)),
            pl.BlockSpec((tm, HEAD_DIM), lambda i: (i % nseq, 0)),
            pl.BlockSpec((1, HEAD_DIM), lambda i: (0, 0)),
            pl.BlockSpec((3, HEAD_DIM), lambda i: (0, 0)),
        ],
        out_specs=(pl.BlockSpec((tm, qd), lambda i: (i, 0)), kvspec, kvspec, kvspec, kvspec,
                   pl.BlockSpec((tm, N_KV_GROUPS * LANES), lambda i: (i, 0))),
        compiler_params=_params("parallel"),
        name="nsa_prep",
    )(proj, gates_pre, cos, sin, q_g.reshape(1, HEAD_DIM), k_g)


def _compress_kernel(kc_ref, vc_ref, w1_ref, w2_ref, pos_ref, kg_ref, cos_ref, sin_ref,
                     ko_ref, vo_ref, *, nchunk):
    def comp(x_ref, i):
        first = jnp.zeros((nchunk, w1_ref.shape[-1]), F32)
        second = jnp.zeros_like(first)
        for l in range(CMP_STRIDE):
            xl = x_ref[pl.ds(l, nchunk, stride=CMP_STRIDE), :]
            xa = (xl + pos_ref[i, l:l + 1, :]).astype(BF16)
            xb = (xl + pos_ref[i, CMP_STRIDE + l:CMP_STRIDE + l + 1, :]).astype(BF16)
            first += jnp.dot(xa, w1_ref[i, l], preferred_element_type=F32)
            second += jnp.dot(xb, w1_ref[i, CMP_STRIDE + l], preferred_element_type=F32)
        pre = first + pltpu.roll(second, nchunk - 1, 0)
        hid = pre * jax.nn.sigmoid(pre)
        return jnp.dot(hid.astype(BF16), w2_ref[i], preferred_element_type=F32)

    k = comp(kc_ref, 0)
    ko_ref[...] = _norm_rope(k, kg_ref[...], cos_ref[...], sin_ref[...]).astype(ko_ref.dtype)
    vo_ref[...] = comp(vc_ref, 1).astype(vo_ref.dtype)


def _compress(proj3, w1, w2, cmp_pos, kg0, cos_c, sin_c):
    batch, seq, _ = proj3.shape
    nchunk = seq // CMP_STRIDE
    hd = HEAD_DIM
    cb = N_HEADS
    out = jax.ShapeDtypeStruct((batch, N_KV_GROUPS, nchunk, hd), BF16)
    ospec = pl.BlockSpec((None, None, nchunk, hd), lambda b, g: (b, g, 0, 0))
    full = lambda a: pl.BlockSpec(a.shape, lambda b, g: (0,) * a.ndim)
    return pl.pallas_call(
        functools.partial(_compress_kernel, nchunk=nchunk),
        out_shape=(out, out),
        grid=(batch, N_KV_GROUPS),
        in_specs=[
            pl.BlockSpec((None, seq, hd), lambda b, g: (b, 0, cb + g)),
            pl.BlockSpec((None, seq, hd), lambda b, g: (b, 0, cb + N_KV_GROUPS + g)),
            full(w1), full(w2), full(cmp_pos), full(kg0), full(cos_c), full(sin_c),
        ],
        out_specs=(ospec, ospec),
        compiler_params=_params("parallel", "parallel"),
        name="nsa_compress",
    )(proj3, proj3, w1, w2, cmp_pos, kg0, cos_c, sin_c)


def _nsa_attn_kernel(q_ref, kc_ref, vc_ref, ks_ref, vs_ref, kw_ref, vw_ref, g_ref, ov_ref, e_ref,
                     o_ref, *, tq, tk, seq, n_sel, n_top, span):
    J = HEADS_PER_GROUP
    hd = HEAD_DIM
    q0 = pl.program_id(2) * tq
    q = q_ref[...]
    qa = jnp.concatenate([q[:, j * hd:(j + 1) * hd] for j in range(J)], axis=0)
    tpos = q0 + lax.broadcasted_iota(I32, (tq, 1), 0)

    def scores(kblk):
        return lax.dot_general(qa, kblk, (((1,), (1,)), ((), ())), preferred_element_type=F32)

    ncp = kc_ref.shape[0]
    n_idx = lax.broadcasted_iota(I32, (1, ncp), 1)
    mask_c = (n_idx * CMP_STRIDE + (CMP_BLOCK - 1)) <= tpos
    s3 = jnp.where(mask_c[None], scores(kc_ref[...]).reshape(J, tq, ncp), NEG_INF)
    m = jnp.max(s3, axis=-1, keepdims=True)
    p = jnp.where(mask_c[None], jnp.exp(s3 - m), 0.0)
    l = jnp.sum(p, axis=-1, keepdims=True)
    pc = p * jnp.where(l > 0.0, 1.0 / l, 0.0)
    o_c = jnp.dot(pc.reshape(J * tq, ncp).astype(BF16), vc_ref[...], preferred_element_type=F32)

    imp = jnp.dot(jnp.sum(pc, axis=0), ov_ref[...], preferred_element_type=F32,
                  precision=lax.Precision.HIGHEST)
    blk = lax.broadcasted_iota(I32, (1, LANES), 1)
    cur = tpos // SEL_BLOCK
    forced = (blk == 0) | (blk == cur) | (blk == cur - 1)
    imp = jnp.where(blk * SEL_BLOCK <= tpos, imp + FORCE_BONUS * forced.astype(F32), NEG_INF)
    rank = jnp.zeros((tq, LANES), F32)
    for mp in range(n_sel):
        col = imp[:, mp:mp + 1]
        rank += jnp.where(blk > mp, (col >= imp).astype(F32), (col > imp).astype(F32))
    sel = ((rank < n_top) & (blk < n_sel)).astype(BF16)

    def sel_chunk(c, carry):
        m_i, l_i, acc = carry
        k0 = pl.multiple_of(c * tk, tk)
        s = scores(ks_ref[pl.ds(k0, tk), :]).reshape(J, tq, tk)
        picked = jnp.dot(sel, e_ref[c], preferred_element_type=F32)
        kpos = k0 + lax.broadcasted_iota(I32, (1, tk), 1)
        msk = ((picked > 0.5) & (kpos <= tpos))[None]
        s = jnp.where(msk, s, NEG_INF)
        m_new = jnp.maximum(m_i, jnp.max(s, axis=-1, keepdims=True))
        alpha = jnp.exp(m_i - m_new)
        pp = jnp.where(msk, jnp.exp(s - m_new), 0.0)
        l_new = alpha * l_i + jnp.sum(pp, axis=-1, keepdims=True)
        pv = jnp.dot(pp.reshape(J * tq, tk).astype(BF16), vs_ref[pl.ds(k0, tk), :],
                     preferred_element_type=F32)
        return m_new, l_new, alpha.reshape(J * tq, 1) * acc + pv

    n_chunks = (q0 + tq + tk - 1) // tk
    init = (jnp.full((J, tq, 1), NEG_INF, F32), jnp.zeros((J, tq, 1), F32), jnp.zeros((J * tq, hd), F32))
    _, l_s, acc_s = lax.fori_loop(0, n_chunks, sel_chunk, init)
    o_s = acc_s * (1.0 / l_s).reshape(J * tq, 1)

    start = pl.multiple_of(jnp.maximum(q0 + tq - span, 0), tq)
    s = scores(kw_ref[pl.ds(start, span), :]).reshape(J, tq, span)
    diff = tpos - (start + lax.broadcasted_iota(I32, (1, span), 1))
    msk = ((diff >= 0) & (diff < WINDOW))[None]
    s = jnp.where(msk, s, NEG_INF)
    m = jnp.max(s, axis=-1, keepdims=True)
    pw = jnp.where(msk, jnp.exp(s - m), 0.0)
    l_w = jnp.sum(pw, axis=-1, keepdims=True)
    o_w = jnp.dot(pw.reshape(J * tq, span).astype(BF16), vw_ref[pl.ds(start, span), :],
                  preferred_element_type=F32) * (1.0 / l_w).reshape(J * tq, 1)

    gt = g_ref[...]
    for j in range(J):
        rows = slice(j * tq, (j + 1) * tq)
        o = (gt[:, 3 * j:3 * j + 1] * o_c[rows] + gt[:, 3 * j + 1:3 * j + 2] * o_s[rows]
             + gt[:, 3 * j + 2:3 * j + 3] * o_w[rows])
        o_ref[:, j * hd:(j + 1) * hd] = o.astype(o_ref.dtype)


def _nsa_attn(qr, kcmp, vcmp, ks, vs, kw, vw, gates, batch, seq):
    hd = HEAD_DIM
    J = HEADS_PER_GROUP
    tq = _tile(seq, TQ)
    tk = _tile(seq, TK_SEL)
    n_sel = seq // SEL_BLOCK
    n_top = min(SEL_TOP_N, n_sel)
    span = min(WINDOW + tq, seq)
    ncp = kcmp.shape[2]
    ci = jnp.arange(ncp, dtype=I32)[:, None] * CMP_STRIDE
    sj = jnp.arange(LANES, dtype=I32)[None, :] * SEL_BLOCK
    ov = jnp.clip(jnp.minimum(ci + CMP_BLOCK, sj + SEL_BLOCK) - jnp.maximum(ci, sj), 0).astype(F32) / CMP_BLOCK
    ov = jnp.where((jnp.arange(ncp)[:, None] < ncp - 1) & (jnp.arange(LANES)[None, :] < n_sel), ov, 0.0)
    kk = jnp.arange(seq, dtype=I32).reshape(seq // tk, 1, tk)
    expand = (kk // SEL_BLOCK == jnp.arange(LANES, dtype=I32)[None, :, None]).astype(BF16)

    r3 = lambda a: a.reshape(batch, seq, a.shape[-1])
    kv_spec = pl.BlockSpec((None, seq, hd), lambda b, g, i: (b, 0, g))
    cmp_spec = pl.BlockSpec((None, None, ncp, hd), lambda b, g, i: (b, g, 0, 0))
    out = pl.pallas_call(
        functools.partial(_nsa_attn_kernel, tq=tq, tk=tk, seq=seq, n_sel=n_sel, n_top=n_top, span=span),
        out_shape=jax.ShapeDtypeStruct((batch, seq, N_HEADS * hd), BF16),
        grid=(batch, N_KV_GROUPS, seq // tq),
        in_specs=[
            pl.BlockSpec((None, tq, J * hd), lambda b, g, i: (b, i, g)),
            cmp_spec, cmp_spec, kv_spec, kv_spec, kv_spec, kv_spec,
            pl.BlockSpec((None, tq, LANES), lambda b, g, i: (b, i, g)),
            pl.BlockSpec(ov.shape, lambda b, g, i: (0, 0)),
            pl.BlockSpec(expand.shape, lambda b, g, i: (0, 0, 0)),
        ],
        out_specs=pl.BlockSpec((None, tq, J * hd), lambda b, g, i: (b, i, g)),
        compiler_params=_params("parallel", "parallel", "parallel"),
        name="nsa_attn",
    )(r3(qr), kcmp, vcmp, r3(ks), r3(vs), r3(kw), r3(vw), r3(gates), ov, expand)
    return out.reshape(batch * seq, N_HEADS * hd)


def _route_kernel(lg_ref, tri_ref, mi_ref, mw_ref, cnt_ref, carry_ref):
    @pl.when(pl.program_id(0) == 0)
    def _():
        carry_ref[...] = jnp.zeros_like(carry_ref)

    lane = lax.broadcasted_iota(I32, lg_ref.shape, 1)
    lg = jnp.where(lane < N_EXPERTS, lg_ref[...], -jnp.inf)
    m1 = jnp.max(lg, axis=-1, keepdims=True)
    i1 = jnp.min(jnp.where(lg == m1, lane, LANES), axis=-1, keepdims=True)
    lg2 = jnp.where(lane == i1, -jnp.inf, lg)
    m2 = jnp.max(lg2, axis=-1, keepdims=True)
    i2 = jnp.min(jnp.where(lg2 == m2, lane, LANES), axis=-1, keepdims=True)
    e2 = jnp.exp(m2 - m1)
    w1 = 1.0 / (1.0 + e2)
    w2 = e2 / (1.0 + e2)
    oh1, oh2 = lane == i1, lane == i2
    onehot = (oh1 | oh2).astype(F32)
    before = jnp.dot(tri_ref[...], onehot.astype(BF16), preferred_element_type=F32) + carry_ref[...]
    pos1 = jnp.sum(jnp.where(oh1, before, 0.0), axis=-1, keepdims=True).astype(I32)
    pos2 = jnp.sum(jnp.where(oh2, before, 0.0), axis=-1, keepdims=True).astype(I32)
    carry_ref[...] += jnp.sum(onehot, axis=0, keepdims=True)
    mi_ref[...] = jnp.where(lane == 0, i1, jnp.where(lane == 1, i2,
                            jnp.where(lane == 2, pos1, jnp.where(lane == 3, pos2, 0))))
    mw_ref[...] = jnp.where(lane == 0, w1, jnp.where(lane == 1, w2, 0.0))
    cnt_ref[...] = carry_ref[...]


def _route(logits):
    t = logits.shape[0]
    tm = _tile(t, 512)
    tri = (jnp.arange(tm)[None, :] < jnp.arange(tm)[:, None]).astype(BF16)
    return pl.pallas_call(
        _route_kernel,
        out_shape=(jax.ShapeDtypeStruct((t, LANES), I32), jax.ShapeDtypeStruct((t, LANES), F32),
                   jax.ShapeDtypeStruct((1, LANES), F32)),
        grid=(t // tm,),
        in_specs=[pl.BlockSpec((tm, LANES), lambda i: (i, 0)), pl.BlockSpec((tm, tm), lambda i: (0, 0))],
        out_specs=(pl.BlockSpec((tm, LANES), lambda i: (i, 0)), pl.BlockSpec((tm, LANES), lambda i: (i, 0)),
                   pl.BlockSpec((1, LANES), lambda i: (0, 0))),
        scratch_shapes=[pltpu.VMEM((1, LANES), F32)],
        compiler_params=_params("arbitrary"),
        name="moe_route",
    )(logits, tri)


def _row_copy(src_hbm, row, dst_ref, slot, sem):
    return pltpu.make_async_copy(src_hbm.at[row], dst_ref.at[slot], sem)


def _gather_kernel(idx_ref, src_hbm, o_ref, sem, *, rows):
    def issue(r, carry):
        _row_copy(src_hbm, idx_ref[0, r], o_ref, r, sem).start()
        return carry

    def drain(r, carry):
        _row_copy(src_hbm, 0, o_ref, r, sem).wait()
        return carry

    lax.fori_loop(0, rows, issue, 0)
    lax.fori_loop(0, rows, drain, 0)


def _gather_rows(src3, idx):
    n = idx.shape[0]
    _, sub, lanes = src3.shape
    tg = _tile(n, TG)
    return pl.pallas_call(
        functools.partial(_gather_kernel, rows=tg),
        out_shape=jax.ShapeDtypeStruct((n, sub, lanes), src3.dtype),
        grid=(n // tg,),
        in_specs=[
            pl.BlockSpec((None, 1, tg), lambda i: (i, 0, 0), memory_space=pltpu.SMEM),
            pl.BlockSpec(memory_space=pl.ANY),
        ],
        out_specs=pl.BlockSpec((tg, sub, lanes), lambda i: (i, 0, 0)),
        scratch_shapes=[pltpu.SemaphoreType.DMA(())],
        compiler_params=_params("arbitrary"),
        name="moe_dispatch",
    )(idx.reshape(n // tg, 1, tg), src3)


def _combine_kernel(i1_ref, i2_ref, y_hbm, w1_ref, w2_ref, o_ref, buf, sem, *, rows):
    def issue(r, carry):
        _row_copy(y_hbm, i1_ref[0, r], buf.at[0], r, sem).start()
        _row_copy(y_hbm, i2_ref[0, r], buf.at[1], r, sem).start()
        return carry

    def drain(r, carry):
        _row_copy(y_hbm, 0, buf.at[0], r, sem).wait()
        _row_copy(y_hbm, 0, buf.at[1], r, sem).wait()
        return carry

    lax.fori_loop(0, rows, issue, 0)
    lax.fori_loop(0, rows, drain, 0)
    o_ref[...] = w1_ref[...] * buf[0].astype(F32) + w2_ref[...] * buf[1].astype(F32)


def _combine_rows(y3, d1, d2, w1, w2):
    t = d1.shape[0]
    _, sub, lanes = y3.shape
    tc = _tile(t, TCMB)
    idx_spec = pl.BlockSpec((None, 1, tc), lambda i: (i, 0, 0), memory_space=pltpu.SMEM)
    w_spec = pl.BlockSpec((tc, 1, lanes), lambda i: (i, 0, 0))
    bcast = lambda w: jnp.broadcast_to(w[:, None, None], (t, 1, lanes))
    return pl.pallas_call(
        functools.partial(_combine_kernel, rows=tc),
        out_shape=jax.ShapeDtypeStruct((t, sub, lanes), F32),
        grid=(t // tc,),
        in_specs=[idx_spec, idx_spec, pl.BlockSpec(memory_space=pl.ANY), w_spec, w_spec],
        out_specs=pl.BlockSpec((tc, sub, lanes), lambda i: (i, 0, 0)),
        scratch_shapes=[pltpu.VMEM((2, tc, sub, lanes), y3.dtype), pltpu.SemaphoreType.DMA(())],
        compiler_params=_params("arbitrary"),
        name="moe_combine",
    )(d1.reshape(t // tc, 1, tc), d2.reshape(t // tc, 1, tc), y3, bcast(w1), bcast(w2))


def _moe_glu_kernel(te_ref, tv_ref, x_ref, wg_ref, wu_ref, o_ref):
    i = pl.program_id(1)

    @pl.when(tv_ref[i] == 1)
    def _():
        x = x_ref[...]
        a = jnp.dot(x, wg_ref[...], preferred_element_type=F32)
        b = jnp.dot(x, wu_ref[...], preferred_element_type=F32)
        o_ref[...] = _silu_mul(a, b).astype(o_ref.dtype)

    @pl.when(tv_ref[i] == 0)
    def _():
        o_ref[...] = jnp.zeros_like(o_ref)


def _moe_down_kernel(te_ref, tv_ref, x_ref, w_ref, o_ref):
    i = pl.program_id(1)

    @pl.when(tv_ref[i] == 1)
    def _():
        o_ref[...] = jnp.dot(x_ref[...], w_ref[...], preferred_element_type=F32).astype(o_ref.dtype)

    @pl.when(tv_ref[i] == 0)
    def _():
        o_ref[...] = jnp.zeros_like(o_ref)


def _moe_mm(kernel, xs, ws, tile_expert, tile_valid, tm, name):
    r, k = xs.shape
    n = ws[0].shape[2]
    tn = _tile(n, TN_MOE)
    w_spec = pl.BlockSpec((None, k, tn), lambda j, i, te, tv: (te[i], 0, j))
    return pl.pallas_call(
        kernel,
        out_shape=jax.ShapeDtypeStruct((r, n), BF16),
        grid_spec=pltpu.PrefetchScalarGridSpec(
            num_scalar_prefetch=2,
            grid=(n // tn, r // tm),
            in_specs=[pl.BlockSpec((tm, k), lambda j, i, te, tv: (i, 0))] + [w_spec] * len(ws),
            out_specs=pl.BlockSpec((tm, tn), lambda j, i, te, tv: (i, j)),
        ),
        compiler_params=_params("parallel", "arbitrary"),
        name=name,
    )(tile_expert, tile_valid, xs, *ws)


def _final_res_kernel(x_ref, f_ref, g_ref, o_ref):
    o_ref[...] = x_ref[...] + g_ref[...] * f_ref[...]


def _final_res(x, f, gate, seq):
    t, d = x.shape
    b = gate.shape[0]
    tm = _tile(seq, TM_ROW)
    spec = pl.BlockSpec((tm, d), lambda i: (i, 0))
    return pl.pallas_call(
        _final_res_kernel,
        out_shape=jax.ShapeDtypeStruct((t, d), F32),
        grid=(t // tm,),
        in_specs=[spec, spec, pl.BlockSpec((None, 1, d), lambda i: (i * tm // seq, 0, 0))],
        out_specs=spec,
        compiler_params=_params("parallel"),
        name="final_res",
    )(x, f, gate.reshape(b, 1, d))


def _moe_ffn(hf, logits, w_gate, w_up, w_down):
    t, d = hf.shape
    n_exp = w_gate.shape[0]
    tm = _tile(t, TM_MOE)
    meta_i, meta_w, counts = _route(logits)
    e1, e2, p1, p2 = meta_i[:, 0], meta_i[:, 1], meta_i[:, 2], meta_i[:, 3]
    cnt = counts[0, :n_exp].astype(I32)
    padded = (cnt + tm - 1) // tm * tm
    ends = jnp.cumsum(padded)
    starts = ends - padded
    d1, d2 = starts[e1] + p1, starts[e2] + p2
    n_rows = 2 * t + n_exp * tm
    tok = jnp.arange(t, dtype=I32)
    src = jnp.zeros((n_rows,), I32).at[d1].set(tok).at[d2].set(tok)
    tile_start = jnp.arange(n_rows // tm, dtype=I32) * tm
    tile_expert = jnp.minimum(jnp.sum(tile_start[:, None] >= ends[None, :], axis=1), n_exp - 1).astype(I32)
    tile_valid = (tile_start < ends[-1]).astype(I32)

    sub = d // LANES
    xs = _gather_rows(hf.reshape(t, sub, LANES), src).reshape(n_rows, d)
    hs = _moe_mm(_moe_glu_kernel, xs, (w_gate, w_up), tile_expert, tile_valid, tm, "moe_glu")
    ys = _moe_mm(_moe_down_kernel, hs, (w_down,), tile_expert, tile_valid, tm, "moe_down")
    out = _combine_rows(ys.reshape(n_rows, sub, LANES), d1, d2, meta_w[:, 0], meta_w[:, 1])
    return out.reshape(t, d)


def kernel(x, c, ada_w, ada_b, norm_mix_g, norm_ffn_g, conv_w_in, conv_w, conv_w_out, nsa_w_in, nsa_w_o,
           nsa_q_norm_g, nsa_k_norm_g, cmp_pos, cmp_w1, cmp_w2, ffn_w_gate, ffn_w_up, ffn_w_down,
           router_w, router_b, moe_w_gate, moe_w_up, moe_w_down):
    batch, seq, d = x.shape
    t = batch * seq
    bf = lambda a: a.astype(BF16)

    mod = _ada_mod(c, ada_w, ada_b)
    sh_m, sc_m, g_m, sh_f, sc_f, g_f = [mod[:, :, i * d:(i + 1) * d] for i in range(6)]
    xt = x.reshape(t, d)

    hm = _norm_mod(xt, norm_mix_g[0], sc_m[0], sh_m[0], seq)
    bcu = _mm(hm, bf(conv_w_in[0]), BF16, "conv_in")
    z = _conv_gate(bcu, conv_w[0], batch, seq)
    xt = _mm_res(z, bf(conv_w_out[0]), xt, g_m[0], seq, "conv_out")
    hf = _norm_mod(xt, norm_ffn_g[0], sc_f[0], sh_f[0], seq)
    hid = _glu(hf, bf(ffn_w_gate[0]), bf(ffn_w_up[0]), "ffn_glu")
    xt = _mm_res(hid, bf(ffn_w_down[0]), xt, g_f[0], seq, "ffn_down", tk=TK_DOWN)

    hm = _norm_mod(xt, norm_mix_g[1], sc_m[1], sh_m[1], seq)
    qkv_w = N_HEADS * HEAD_DIM + 6 * KV_WIDTH
    w_in = nsa_w_in[0]
    proj = _mm(hm, bf(w_in[:, :qkv_w]), F32, "nsa_in")
    w_gates = jnp.pad(w_in[:, qkv_w:], ((0, 0), (0, LANES - 3 * N_HEADS)))
    gates_pre = _mm(hm, bf(w_gates), F32, "nsa_gates")
    pos = jnp.arange(seq, dtype=I32)
    cos, sin = _rope_tables(pos)
    qr, ks, vs, kw, vw, gates = _nsa_prep(proj, gates_pre, cos, sin, nsa_q_norm_g[0], nsa_k_norm_g[0], seq)
    cmp_end = jnp.arange(seq // CMP_STRIDE, dtype=I32) * CMP_STRIDE + CMP_BLOCK - 1
    cos_c, sin_c = _rope_tables(cmp_end)
    kcmp, vcmp = _compress(proj.reshape(batch, seq, qkv_w), bf(cmp_w1[0]), bf(cmp_w2[0]), cmp_pos[0],
                           nsa_k_norm_g[0, 0:1], cos_c, sin_c)
    attn = _nsa_attn(qr, kcmp, vcmp, ks, vs, kw, vw, gates, batch, seq)
    xt = _mm_res(attn, bf(nsa_w_o[0]), xt, g_m[1], seq, "nsa_out")

    n_exp = router_w.shape[2]
    rw = jnp.pad(router_w[0], ((0, 0), (0, LANES - n_exp)))
    rb = jnp.pad(router_b[0], (0, LANES - n_exp)).reshape(1, LANES)
    hf, logits = _norm_mod(xt, norm_ffn_g[1], sc_f[1], sh_f[1], seq, router=(rw, rb))
    f = _moe_ffn(hf, logits, bf(moe_w_gate[0]), bf(moe_w_up[0]), bf(moe_w_down[0]))
    out = _final_res(xt, f, g_f[1], seq)
    return out.reshape(batch, seq, d)
```

```python
import functools

import jax
import jax.numpy as jnp
from jax import lax
from jax.experimental import pallas as pl
from jax.experimental.pallas import tpu as pltpu

F32 = jnp.float32
BF16 = jnp.bfloat16
I32 = jnp.int32

EPS = 1e-6
NEG_INF = -1e30
FORCE_BONUS = 1e4
ROPE_THETA = 10000.0
LOG2E = 1.4426950408889634

CONV_WIDTH = 3
N_HEADS = 32
HEAD_DIM = 128
N_KV_GROUPS = 4
HEADS_PER_GROUP = N_HEADS // N_KV_GROUPS
KV_WIDTH = N_KV_GROUPS * HEAD_DIM
CMP_BLOCK = 32
CMP_STRIDE = 16
SEL_BLOCK = 64
SEL_TOP_N = 16
WINDOW = 512
N_EXPERTS = 8

LANES = 128
VMEM_LIMIT_V7X = 56 * 1024 * 1024

TM = 1024
TN = 1024
TN_RES = 512
TN_GLU = 256
TK_DOWN = 5504
TM_ROW = 512
TC_CONV = 512
TQ = 128
TK_SEL = 512
TM_MOE = 512
TN_MOE = 512
TG = 256
TCMB = 128


def _params(*sem):
    return pltpu.CompilerParams(dimension_semantics=sem, vmem_limit_bytes=VMEM_LIMIT_V7X)


def _tile(full, want):
    return want if full % want == 0 else full


def _ada_kernel(c_ref, w_ref, b_ref, o_ref):
    c = c_ref[...]
    cond = c * jax.nn.sigmoid(c)
    acc = jnp.dot(cond.astype(BF16), w_ref[...].astype(BF16), preferred_element_type=F32)
    o_ref[...] = acc + b_ref[...]


def _ada_mod(c, ada_w, ada_b):
    depth, d, n = ada_w.shape
    b = c.shape[0]
    tn = _tile(n, 512)
    return pl.pallas_call(
        _ada_kernel,
        out_shape=jax.ShapeDtypeStruct((depth, b, n), F32),
        grid=(depth, n // tn),
        in_specs=[
            pl.BlockSpec((b, d), lambda l, j: (0, 0)),
            pl.BlockSpec((None, d, tn), lambda l, j: (l, 0, j)),
            pl.BlockSpec((None, 1, tn), lambda l, j: (l, 0, j)),
        ],
        out_specs=pl.BlockSpec((None, b, tn), lambda l, j: (l, 0, j)),
        compiler_params=_params("parallel", "parallel"),
        name="ada_mod",
    )(c, ada_w, ada_b.reshape(depth, 1, n))


def _norm_mod_value(x_ref, g_ref, sc_ref, sh_ref):
    x = x_ref[...]
    ms = jnp.mean(x * x, axis=-1, keepdims=True)
    y = x * lax.rsqrt(ms + EPS) * g_ref[...]
    return y * (1.0 + sc_ref[...]) + sh_ref[...]


def _norm_mod_kernel(x_ref, g_ref, sc_ref, sh_ref, o_ref):
    o_ref[...] = _norm_mod_value(x_ref, g_ref, sc_ref, sh_ref).astype(o_ref.dtype)


def _norm_mod_router_kernel(x_ref, g_ref, sc_ref, sh_ref, rw_ref, rb_ref, o_ref, lg_ref):
    h = _norm_mod_value(x_ref, g_ref, sc_ref, sh_ref)
    o_ref[...] = h.astype(o_ref.dtype)
    lg_ref[...] = jnp.dot(h, rw_ref[...], preferred_element_type=F32,
                          precision=lax.Precision.HIGHEST) + rb_ref[...]


def _norm_mod(x, g, sc, sh, seq, router=None):
    t, d = x.shape
    b = sc.shape[0]
    tm = _tile(seq, TM_ROW)
    row_specs = [
        pl.BlockSpec((tm, d), lambda i: (i, 0)),
        pl.BlockSpec((1, d), lambda i: (0, 0)),
        pl.BlockSpec((None, 1, d), lambda i: (i * tm // seq, 0, 0)),
        pl.BlockSpec((None, 1, d), lambda i: (i * tm // seq, 0, 0)),
    ]
    args = [x, g.reshape(1, d), sc.reshape(b, 1, d), sh.reshape(b, 1, d)]
    if router is None:
        return pl.pallas_call(
            _norm_mod_kernel,
            out_shape=jax.ShapeDtypeStruct((t, d), BF16),
            grid=(t // tm,),
            in_specs=row_specs,
            out_specs=pl.BlockSpec((tm, d), lambda i: (i, 0)),
            compiler_params=_params("parallel"),
            name="norm_mod",
        )(*args)
    rw, rb = router
    return pl.pallas_call(
        _norm_mod_router_kernel,
        out_shape=(jax.ShapeDtypeStruct((t, d), F32), jax.ShapeDtypeStruct((t, LANES), F32)),
        grid=(t // tm,),
        in_specs=row_specs + [
            pl.BlockSpec((d, LANES), lambda i: (0, 0)),
            pl.BlockSpec((1, LANES), lambda i: (0, 0)),
        ],
        out_specs=(pl.BlockSpec((tm, d), lambda i: (i, 0)), pl.BlockSpec((tm, LANES), lambda i: (i, 0))),
        compiler_params=_params("parallel"),
        name="norm_mod_router",
    )(*args, rw, rb)


def _mm_kernel(x_ref, w_ref, o_ref):
    o_ref[...] = jnp.dot(x_ref[...], w_ref[...], preferred_element_type=F32).astype(o_ref.dtype)


def _mm(x, w, out_dtype, name, tm=TM, tn=TN):
    m, k = x.shape
    n = w.shape[1]
    tm, tn = _tile(m, tm), _tile(n, tn)
    return pl.pallas_call(
        _mm_kernel,
        out_shape=jax.ShapeDtypeStruct((m, n), out_dtype),
        grid=(m // tm, n // tn),
        in_specs=[pl.BlockSpec((tm, k), lambda i, j: (i, 0)), pl.BlockSpec((k, tn), lambda i, j: (0, j))],
        out_specs=pl.BlockSpec((tm, tn), lambda i, j: (i, j)),
        compiler_params=_params("parallel", "parallel"),
        name=name,
    )(x, w)


def _mm_res_kernel(x_ref, w_ref, res_ref, g_ref, o_ref, *acc, nk):
    part = jnp.dot(x_ref[...], w_ref[...], preferred_element_type=F32)
    if nk == 1:
        o_ref[...] = res_ref[...] + g_ref[...] * part
        return
    acc_ref, = acc
    k = pl.program_id(2)

    @pl.when(k == 0)
    def _():
        acc_ref[...] = part

    @pl.when(k > 0)
    def _():
        acc_ref[...] += part

    @pl.when(k == nk - 1)
    def _():
        o_ref[...] = res_ref[...] + g_ref[...] * acc_ref[...]


def _mm_res(x, w, res, gate, seq, name, tm=TM, tn=TN_RES, tk=None):
    m, k = x.shape
    n = w.shape[1]
    b = gate.shape[0]
    tm, tn = _tile(seq, tm), _tile(n, tn)
    tk = k if tk is None else _tile(k, tk)
    nk = k // tk
    scratch = [] if nk == 1 else [pltpu.VMEM((tm, tn), F32)]
    return pl.pallas_call(
        functools.partial(_mm_res_kernel, nk=nk),
        out_shape=jax.ShapeDtypeStruct((m, n), F32),
        grid=(m // tm, n // tn, nk),
        in_specs=[
            pl.BlockSpec((tm, tk), lambda i, j, kk: (i, kk)),
            pl.BlockSpec((tk, tn), lambda i, j, kk: (kk, j)),
            pl.BlockSpec((tm, tn), lambda i, j, kk: (i, j)),
            pl.BlockSpec((None, 1, tn), lambda i, j, kk: (i * tm // seq, 0, j)),
        ],
        out_specs=pl.BlockSpec((tm, tn), lambda i, j, kk: (i, j)),
        scratch_shapes=scratch,
        compiler_params=_params("parallel", "parallel", "arbitrary"),
        name=name,
    )(x, w, res, gate.reshape(b, 1, n))


def _silu_mul(a, b):
    return (a * jax.nn.sigmoid(a)) * b


def _ws_glu_kernel(te_ref, tv_ref, tf_ref, x_ref, wg_ref, wu_ref, o_ref, wg_bf, wu_bf):
    i = pl.program_id(1)

    @pl.when(tf_ref[i] == 1)
    def _():
        wg_bf[...] = wg_ref[...].astype(BF16)
        wu_bf[...] = wu_ref[...].astype(BF16)

    @pl.when(tv_ref[i] == 1)
    def _():
        x = x_ref[...]
        a = jnp.dot(x, wg_bf[...], preferred_element_type=F32)
        b = jnp.dot(x, wu_bf[...], preferred_element_type=F32)
        o_ref[...] = _silu_mul(a, b).astype(o_ref.dtype)

    @pl.when(tv_ref[i] == 0)
    def _():
        o_ref[...] = jnp.zeros_like(o_ref)


def _ws_mm_kernel(te_ref, tv_ref, tf_ref, x_ref, w_ref, o_ref, w_bf):
    i = pl.program_id(1)

    @pl.when(tf_ref[i] == 1)
    def _():
        w_bf[...] = w_ref[...].astype(BF16)

    @pl.when(tv_ref[i] == 1)
    def _():
        o_ref[...] = jnp.dot(x_ref[...], w_bf[...], preferred_element_type=F32).astype(o_ref.dtype)

    @pl.when(tv_ref[i] == 0)
    def _():
        o_ref[...] = jnp.zeros_like(o_ref)


def _ws_mm(kernel, xs, ws, sched, tm, tn, out_dtype, name):
    r, k = xs.shape
    n = ws[0].shape[2]
    tn = _tile(n, tn)
    w_spec = pl.BlockSpec((None, k, tn), lambda j, i, te, tv, tf: (te[i], 0, j))
    return pl.pallas_call(
        kernel,
        out_shape=jax.ShapeDtypeStruct((r, n), out_dtype),
        grid_spec=pltpu.PrefetchScalarGridSpec(
            num_scalar_prefetch=3,
            grid=(n // tn, r // tm),
            in_specs=[pl.BlockSpec((tm, k), lambda j, i, te, tv, tf: (i, 0))] + [w_spec] * len(ws),
            out_specs=pl.BlockSpec((tm, tn), lambda j, i, te, tv, tf: (i, j)),
            scratch_shapes=[pltpu.VMEM((k, tn), BF16)] * len(ws),
        ),
        compiler_params=_params("parallel", "arbitrary"),
        name=name,
    )(*sched, xs, *ws)


def _dense_sched(n_tiles):
    first = (jnp.arange(n_tiles, dtype=I32) == 0).astype(I32)
    return jnp.zeros((n_tiles,), I32), jnp.ones((n_tiles,), I32), first


def _conv_gate_kernel(b_ref, c_ref, u_ref, w_ref, o_ref):
    v = c_ref[...].astype(F32) * u_ref[...].astype(F32)
    rows = lax.broadcasted_iota(I32, v.shape, 0)
    v1 = jnp.where(rows >= 1, pltpu.roll(v, 1, 0), 0.0)
    v2 = jnp.where(rows >= 2, pltpu.roll(v, 2, 0), 0.0)
    w = w_ref[...]
    y = w[0:1, :] * v2 + w[1:2, :] * v1 + w[2:3, :] * v
    o_ref[...] = (b_ref[...].astype(F32) * y).astype(o_ref.dtype)


def _conv_gate(bcu, conv_w, batch, seq):
    d = conv_w.shape[1]
    tc = _tile(d, TC_CONV)
    nc = d // tc
    bcu3 = bcu.reshape(batch, seq, 3 * d)
    out = pl.pallas_call(
        _conv_gate_kernel,
        out_shape=jax.ShapeDtypeStruct((batch, seq, d), BF16),
        grid=(batch, nc),
        in_specs=[
            pl.BlockSpec((None, seq, tc), lambda b, j: (b, 0, j)),
            pl.BlockSpec((None, seq, tc), lambda b, j: (b, 0, nc + j)),
            pl.BlockSpec((None, seq, tc), lambda b, j: (b, 0, 2 * nc + j)),
            pl.BlockSpec((CONV_WIDTH, tc), lambda b, j: (0, j)),
        ],
        out_specs=pl.BlockSpec((None, seq, tc), lambda b, j: (b, 0, j)),
        compiler_params=_params("parallel", "parallel"),
        name="conv_gate",
    )(bcu3, bcu3, bcu3, conv_w)
    return out.reshape(batch * seq, d)


def _rope_tables(pos):
    half = HEAD_DIM // 2
    inv = ROPE_THETA ** (-jnp.arange(half, dtype=F32) / half)
    ang = pos.astype(F32)[:, None] * inv[None, :]
    cos, sin = jnp.cos(ang), jnp.sin(ang)
    return jnp.concatenate([cos, cos], axis=-1), jnp.concatenate([-sin, sin], axis=-1)


def _norm_rope(x, g, cos, sin_signed):
    ms = jnp.mean(x * x, axis=-1, keepdims=True)
    xn = x * lax.rsqrt(ms + EPS) * g
    return xn * cos + pltpu.roll(xn, HEAD_DIM // 2, 1) * sin_signed


def _nsa_prep_kernel(p_ref, gp_ref, cos_ref, sin_ref, qg_ref, kg_ref,
                     q_o, ks_o, vs_o, kw_o, vw_o, g_o, *, scale):
    cos, sin = cos_ref[...], sin_ref[...]
    hd = HEAD_DIM
    qg = qg_ref[...]
    for h in range(N_HEADS):
        x = p_ref[:, h * hd:(h + 1) * hd]
        q_o[:, h * hd:(h + 1) * hd] = (_norm_rope(x, qg, cos, sin) * scale).astype(q_o.dtype)
    kv0 = N_HEADS * hd
    for g in range(N_KV_GROUPS):
        def col(i, g=g):
            c0 = kv0 + i * KV_WIDTH + g * hd
            return p_ref[:, c0:c0 + hd]
        sl = slice(g * hd, (g + 1) * hd)
        ks_o[:, sl] = _norm_rope(col(2), kg_ref[1:2, :], cos, sin).astype(ks_o.dtype)
        vs_o[:, sl] = col(3).astype(vs_o.dtype)
        kw_o[:, sl] = _norm_rope(col(4), kg_ref[2:3, :], cos, sin).astype(kw_o.dtype)
        vw_o[:, sl] = col(5).astype(vw_o.dtype)
    sig = jax.nn.sigmoid(gp_ref[...])
    per_group = 3 * HEADS_PER_GROUP
    for g in range(N_KV_GROUPS):
        shift = (LANES - g * per_group) % LANES
        g_o[:, g * LANES:(g + 1) * LANES] = sig if shift == 0 else pltpu.roll(sig, shift, 1)


def _nsa_prep(proj, gates_pre, cos, sin, q_g, k_g, seq):
    t = proj.shape[0]
    tm = _tile(seq, 256)
    nseq = seq // tm
    qd = N_HEADS * HEAD_DIM
    kvspec = pl.BlockSpec((tm, KV_WIDTH), lambda i: (i, 0))
    kvshape = jax.ShapeDtypeStruct((t, KV_WIDTH), BF16)
    return pl.pallas_call(
        functools.partial(_nsa_prep_kernel, scale=HEAD_DIM ** -0.5 * LOG2E),
        out_shape=(jax.ShapeDtypeStruct((t, qd), BF16), kvshape, kvshape, kvshape, kvshape,
                   jax.ShapeDtypeStruct((t, N_KV_GROUPS * LANES), F32)),
        grid=(t // tm,),
        in_specs=[
            pl.BlockSpec((tm, proj.shape[1]), lambda i: (i, 0)),
            pl.BlockSpec((tm, LANES), lambda i: (i, 0)),
            pl.BlockSpec((tm, HEAD_DIM), lambda i: (i % nseq, 0)),
            pl.BlockSpec((tm, HEAD_DIM), lambda i: (i % nseq, 0)),
            pl.BlockSpec((1, HEAD_DIM), lambda i: (0, 0)),
            pl.BlockSpec((3, HEAD_DIM), lambda i: (0, 0)),
        ],
        out_specs=(pl.BlockSpec((tm, qd), lambda i: (i, 0)), kvspec, kvspec, kvspec, kvspec,
                   pl.BlockSpec((tm, N_KV_GROUPS * LANES), lambda i: (i, 0))),
        compiler_params=_params("parallel"),
        name="nsa_prep",
    )(proj, gates_pre, cos, sin, q_g.reshape(1, HEAD_DIM), k_g)


def _compress_kernel(kc_ref, vc_ref, w1_ref, w2_ref, pos_ref, kg_ref, cos_ref, sin_ref,
                     ko_ref, vo_ref, *, nchunk):
    def comp(x_ref, i):
        first = jnp.zeros((nchunk, w1_ref.shape[-1]), F32)
        second = jnp.zeros_like(first)
        for l in range(CMP_STRIDE):
            xl = x_ref[pl.ds(l, nchunk, stride=CMP_STRIDE), :]
            xa = (xl + pos_ref[i, l:l + 1, :]).astype(BF16)
            xb = (xl + pos_ref[i, CMP_STRIDE + l:CMP_STRIDE + l + 1, :]).astype(BF16)
            first += jnp.dot(xa, w1_ref[i, l], preferred_element_type=F32)
            second += jnp.dot(xb, w1_ref[i, CMP_STRIDE + l], preferred_element_type=F32)
        pre = first + pltpu.roll(second, nchunk - 1, 0)
        hid = pre * jax.nn.sigmoid(pre)
        return jnp.dot(hid.astype(BF16), w2_ref[i], preferred_element_type=F32)

    k = comp(kc_ref, 0)
    ko_ref[...] = _norm_rope(k, kg_ref[...], cos_ref[...], sin_ref[...]).astype(ko_ref.dtype)
    vo_ref[...] = comp(vc_ref, 1).astype(vo_ref.dtype)


def _compress(proj3, w1, w2, cmp_pos, kg0, cos_c, sin_c):
    batch, seq, _ = proj3.shape
    nchunk = seq // CMP_STRIDE
    hd = HEAD_DIM
    cb = N_HEADS
    out = jax.ShapeDtypeStruct((batch, N_KV_GROUPS, nchunk, hd), BF16)
    ospec = pl.BlockSpec((None, None, nchunk, hd), lambda b, g: (b, g, 0, 0))
    full = lambda a: pl.BlockSpec(a.shape, lambda b, g: (0,) * a.ndim)
    return pl.pallas_call(
        functools.partial(_compress_kernel, nchunk=nchunk),
        out_shape=(out, out),
        grid=(batch, N_KV_GROUPS),
        in_specs=[
            pl.BlockSpec((None, seq, hd), lambda b, g: (b, 0, cb + g)),
            pl.BlockSpec((None, seq, hd), lambda b, g: (b, 0, cb + N_KV_GROUPS + g)),
            full(w1), full(w2), full(cmp_pos), full(kg0), full(cos_c), full(sin_c),
        ],
        out_specs=(ospec, ospec),
        compiler_params=_params("parallel", "parallel"),
        name="nsa_compress",
    )(proj3, proj3, w1, w2, cmp_pos, kg0, cos_c, sin_c)


def _nsa_attn_kernel(q_ref, kc_ref, vc_ref, ks_ref, vs_ref, kw_ref, vw_ref, g_ref, ov_ref, e_ref,
                     o_ref, *, tq, tk, seq, n_sel, n_top, span):
    J = HEADS_PER_GROUP
    hd = HEAD_DIM
    q0 = pl.program_id(2) * tq
    q = q_ref[...]
    qa = jnp.concatenate([q[:, j * hd:(j + 1) * hd] for j in range(J)], axis=0)
    tpos = q0 + lax.broadcasted_iota(I32, (tq, 1), 0)

    def scores(kblk):
        return lax.dot_general(qa, kblk, (((1,), (1,)), ((), ())), preferred_element_type=F32)

    ncp = kc_ref.shape[0]
    n_idx = lax.broadcasted_iota(I32, (1, ncp), 1)
    mask_c = (n_idx * CMP_STRIDE + (CMP_BLOCK - 1)) <= tpos
    s3 = jnp.where(mask_c[None], scores(kc_ref[...]).reshape(J, tq, ncp), NEG_INF)
    m = jnp.max(s3, axis=-1, keepdims=True)
    p = jnp.where(mask_c[None], jnp.exp2(s3 - m), 0.0)
    l = jnp.sum(p, axis=-1, keepdims=True)
    pc = p * jnp.where(l > 0.0, 1.0 / l, 0.0)
    o_c = jnp.dot(pc.reshape(J * tq, ncp).astype(BF16), vc_ref[...], preferred_element_type=F32)

    imp = jnp.dot(jnp.sum(pc, axis=0), ov_ref[...], preferred_element_type=F32,
                  precision=lax.Precision.HIGHEST)
    blk = lax.broadcasted_iota(I32, (1, LANES), 1)
    cur = tpos // SEL_BLOCK
    forced = (blk == 0) | (blk == cur) | (blk == cur - 1)
    imp = jnp.where(blk * SEL_BLOCK <= tpos, imp + FORCE_BONUS * forced.astype(F32), NEG_INF)
    rank = jnp.zeros((tq, LANES), F32)
    for mp in range(n_sel):
        col = imp[:, mp:mp + 1]
        rank += jnp.where(blk > mp, (col >= imp).astype(F32), (col > imp).astype(F32))
    sel = ((rank < n_top) & (blk < n_sel)).astype(BF16)

    def sel_chunk(c, carry):
        m_i, l_i, acc = carry
        k0 = pl.multiple_of(c * tk, tk)
        s = scores(ks_ref[pl.ds(k0, tk), :]).reshape(J, tq, tk)
        picked = jnp.dot(sel, e_ref[c], preferred_element_type=F32)
        kpos = k0 + lax.broadcasted_iota(I32, (1, tk), 1)
        s = s + jnp.where((picked > 0.5) & (kpos <= tpos), 0.0, NEG_INF)[None]
        m_new = jnp.maximum(m_i, jnp.max(s, axis=-1, keepdims=True))
        alpha = jnp.exp2(m_i - m_new)
        pp = jnp.exp2(s - m_new)
        l_new = alpha * l_i + jnp.sum(pp, axis=-1, keepdims=True)
        pv = jnp.dot(pp.reshape(J * tq, tk).astype(BF16), vs_ref[pl.ds(k0, tk), :],
                     preferred_element_type=F32)
        return m_new, l_new, alpha.reshape(J * tq, 1) * acc + pv

    n_chunks = (q0 + tq + tk - 1) // tk
    init = (jnp.full((J, tq, 1), NEG_INF, F32), jnp.zeros((J, tq, 1), F32), jnp.zeros((J * tq, hd), F32))
    _, l_s, acc_s = lax.fori_loop(0, n_chunks, sel_chunk, init)
    o_s = acc_s * (1.0 / l_s).reshape(J * tq, 1)

    start = pl.multiple_of(jnp.maximum(q0 + tq - span, 0), tq)
    s = scores(kw_ref[pl.ds(start, span), :]).reshape(J, tq, span)
    diff = tpos - (start + lax.broadcasted_iota(I32, (1, span), 1))
    s = s + jnp.where((diff >= 0) & (diff < WINDOW), 0.0, NEG_INF)[None]
    m = jnp.max(s, axis=-1, keepdims=True)
    pw = jnp.exp2(s - m)
    l_w = jnp.sum(pw, axis=-1, keepdims=True)
    o_w = jnp.dot(pw.reshape(J * tq, span).astype(BF16), vw_ref[pl.ds(start, span), :],
                  preferred_element_type=F32) * (1.0 / l_w).reshape(J * tq, 1)

    gt = g_ref[...]
    for j in range(J):
        rows = slice(j * tq, (j + 1) * tq)
        o = (gt[:, 3 * j:3 * j + 1] * o_c[rows] + gt[:, 3 * j + 1:3 * j + 2] * o_s[rows]
             + gt[:, 3 * j + 2:3 * j + 3] * o_w[rows])
        o_ref[:, j * hd:(j + 1) * hd] = o.astype(o_ref.dtype)


def _nsa_attn(qr, kcmp, vcmp, ks, vs, kw, vw, gates, batch, seq):
    hd = HEAD_DIM
    J = HEADS_PER_GROUP
    tq = _tile(seq, TQ)
    tk = _tile(seq, TK_SEL)
    n_sel = seq // SEL_BLOCK
    n_top = min(SEL_TOP_N, n_sel)
    span = min(WINDOW + tq, seq)
    ncp = kcmp.shape[2]
    ci = jnp.arange(ncp, dtype=I32)[:, None] * CMP_STRIDE
    sj = jnp.arange(LANES, dtype=I32)[None, :] * SEL_BLOCK
    ov = jnp.clip(jnp.minimum(ci + CMP_BLOCK, sj + SEL_BLOCK) - jnp.maximum(ci, sj), 0).astype(F32) / CMP_BLOCK
    ov = jnp.where((jnp.arange(ncp)[:, None] < ncp - 1) & (jnp.arange(LANES)[None, :] < n_sel), ov, 0.0)
    kk = jnp.arange(seq, dtype=I32).reshape(seq // tk, 1, tk)
    expand = (kk // SEL_BLOCK == jnp.arange(LANES, dtype=I32)[None, :, None]).astype(BF16)

    r3 = lambda a: a.reshape(batch, seq, a.shape[-1])
    kv_spec = pl.BlockSpec((None, seq, hd), lambda b, g, i: (b, 0, g))
    cmp_spec = pl.BlockSpec((None, None, ncp, hd), lambda b, g, i: (b, g, 0, 0))
    out = pl.pallas_call(
        functools.partial(_nsa_attn_kernel, tq=tq, tk=tk, seq=seq, n_sel=n_sel, n_top=n_top, span=span),
        out_shape=jax.ShapeDtypeStruct((batch, seq, N_HEADS * hd), BF16),
        grid=(batch, N_KV_GROUPS, seq // tq),
        in_specs=[
            pl.BlockSpec((None, tq, J * hd), lambda b, g, i: (b, i, g)),
            cmp_spec, cmp_spec, kv_spec, kv_spec, kv_spec, kv_spec,
            pl.BlockSpec((None, tq, LANES), lambda b, g, i: (b, i, g)),
            pl.BlockSpec(ov.shape, lambda b, g, i: (0, 0)),
            pl.BlockSpec(expand.shape, lambda b, g, i: (0, 0, 0)),
        ],
        out_specs=pl.BlockSpec((None, tq, J * hd), lambda b, g, i: (b, i, g)),
        compiler_params=_params("parallel", "parallel", "parallel"),
        name="nsa_attn",
    )(r3(qr), kcmp, vcmp, r3(ks), r3(vs), r3(kw), r3(vw), r3(gates), ov, expand)
    return out.reshape(batch * seq, N_HEADS * hd)


def _route_kernel(lg_ref, tri_ref, mi_ref, mw_ref, cnt_ref, carry_ref):
    @pl.when(pl.program_id(0) == 0)
    def _():
        carry_ref[...] = jnp.zeros_like(carry_ref)

    lane = lax.broadcasted_iota(I32, lg_ref.shape, 1)
    lg = jnp.where(lane < N_EXPERTS, lg_ref[...], -jnp.inf)
    m1 = jnp.max(lg, axis=-1, keepdims=True)
    i1 = jnp.min(jnp.where(lg == m1, lane, LANES), axis=-1, keepdims=True)
    lg2 = jnp.where(lane == i1, -jnp.inf, lg)
    m2 = jnp.max(lg2, axis=-1, keepdims=True)
    i2 = jnp.min(jnp.where(lg2 == m2, lane, LANES), axis=-1, keepdims=True)
    e2 = jnp.exp(m2 - m1)
    w1 = 1.0 / (1.0 + e2)
    w2 = e2 / (1.0 + e2)
    oh1, oh2 = lane == i1, lane == i2
    onehot = (oh1 | oh2).astype(F32)
    before = jnp.dot(tri_ref[...], onehot.astype(BF16), preferred_element_type=F32) + carry_ref[...]
    pos1 = jnp.sum(jnp.where(oh1, before, 0.0), axis=-1, keepdims=True).astype(I32)
    pos2 = jnp.sum(jnp.where(oh2, before, 0.0), axis=-1, keepdims=True).astype(I32)
    carry_ref[...] += jnp.sum(onehot, axis=0, keepdims=True)
    mi_ref[...] = jnp.where(lane == 0, i1, jnp.where(lane == 1, i2,
                            jnp.where(lane == 2, pos1, jnp.where(lane == 3, pos2, 0))))
    mw_ref[...] = jnp.where(lane == 0, w1, jnp.where(lane == 1, w2, 0.0))
    cnt_ref[...] = carry_ref[...]


def _route(logits):
    t = logits.shape[0]
    tm = _tile(t, 512)
    tri = (jnp.arange(tm)[None, :] < jnp.arange(tm)[:, None]).astype(BF16)
    return pl.pallas_call(
        _route_kernel,
        out_shape=(jax.ShapeDtypeStruct((t, LANES), I32), jax.ShapeDtypeStruct((t, LANES), F32),
                   jax.ShapeDtypeStruct((1, LANES), F32)),
        grid=(t // tm,),
        in_specs=[pl.BlockSpec((tm, LANES), lambda i: (i, 0)), pl.BlockSpec((tm, tm), lambda i: (0, 0))],
        out_specs=(pl.BlockSpec((tm, LANES), lambda i: (i, 0)), pl.BlockSpec((tm, LANES), lambda i: (i, 0)),
                   pl.BlockSpec((1, LANES), lambda i: (0, 0))),
        scratch_shapes=[pltpu.VMEM((1, LANES), F32)],
        compiler_params=_params("arbitrary"),
        name="moe_route",
    )(logits, tri)


def _row_copy(src_hbm, row, buf, slot, r, sem):
    return pltpu.make_async_copy(src_hbm.at[pl.ds(row, 1), :], buf.at[slot, pl.ds(r, 1), :], sem.at[slot])


def _issue_rows(idx_ref, src_hbm, buf, slot, base, sem, rows):
    def body(r, carry):
        _row_copy(src_hbm, idx_ref[0, r], buf, slot, base + r, sem).start()
        return carry

    lax.fori_loop(0, rows, body, 0, unroll=8)


def _wait_rows(src_hbm, buf, slot, sem, rows):
    def body(r, carry):
        _row_copy(src_hbm, 0, buf, slot, r, sem).wait()
        return carry

    lax.fori_loop(0, rows, body, 0, unroll=8)


def _dispatch_kernel(idx_ref, nxt_ref, src_hbm, o_ref, buf, sem, *, rows):
    i = pl.program_id(0)
    slot = i % 2

    @pl.when(i == 0)
    def _():
        _issue_rows(idx_ref, src_hbm, buf, 0, 0, sem, rows)

    @pl.when(i + 1 < pl.num_programs(0))
    def _():
        _issue_rows(nxt_ref, src_hbm, buf, 1 - slot, 0, sem, rows)

    _wait_rows(src_hbm, buf, slot, sem, rows)
    o_ref[...] = buf[slot].astype(o_ref.dtype)


def _dispatch_rows(src, idx):
    n = idx.shape[0]
    d = src.shape[1]
    tg = _tile(n, TG)
    steps = n // tg
    smem = lambda f: pl.BlockSpec((None, 1, tg), f, memory_space=pltpu.SMEM)
    idx3 = idx.reshape(steps, 1, tg)
    return pl.pallas_call(
        functools.partial(_dispatch_kernel, rows=tg),
        out_shape=jax.ShapeDtypeStruct((n, d), BF16),
        grid=(steps,),
        in_specs=[
            smem(lambda i: (i, 0, 0)),
            smem(lambda i: (jnp.minimum(i + 1, steps - 1), 0, 0)),
            pl.BlockSpec(memory_space=pl.ANY),
        ],
        out_specs=pl.BlockSpec((tg, d), lambda i: (i, 0)),
        scratch_shapes=[pltpu.VMEM((2, tg, d), F32), pltpu.SemaphoreType.DMA((2,))],
        compiler_params=_params("arbitrary"),
        name="moe_dispatch",
    )(idx3, idx3, src)


def _combine_kernel(a_ref, b_ref, na_ref, nb_ref, y_hbm, x_ref, g_ref, mw_ref, o_ref, buf, sem, *, rows):
    i = pl.program_id(0)
    slot = i % 2

    def issue(first_ref, second_ref, s):
        _issue_rows(first_ref, y_hbm, buf, s, 0, sem, rows)
        _issue_rows(second_ref, y_hbm, buf, s, rows, sem, rows)

    @pl.when(i == 0)
    def _():
        issue(a_ref, b_ref, 0)

    @pl.when(i + 1 < pl.num_programs(0))
    def _():
        issue(na_ref, nb_ref, 1 - slot)

    _wait_rows(y_hbm, buf, slot, sem, 2 * rows)
    mw = mw_ref[...]
    f = mw[:, 0:1] * buf[slot, 0:rows, :] + mw[:, 1:2] * buf[slot, rows:2 * rows, :]
    o_ref[...] = x_ref[...] + g_ref[...] * f


def _combine_rows(ys, d1, d2, meta_w, x, gate, seq):
    t, d = x.shape
    b = gate.shape[0]
    tc = _tile(seq, TCMB)
    steps = t // tc
    smem = lambda f: pl.BlockSpec((None, 1, tc), f, memory_space=pltpu.SMEM)
    cur, nxt = (lambda i: (i, 0, 0)), (lambda i: (jnp.minimum(i + 1, steps - 1), 0, 0))
    a3, b3 = d1.reshape(steps, 1, tc), d2.reshape(steps, 1, tc)
    return pl.pallas_call(
        functools.partial(_combine_kernel, rows=tc),
        out_shape=jax.ShapeDtypeStruct((t, d), F32),
        grid=(steps,),
        in_specs=[
            smem(cur), smem(cur), smem(nxt), smem(nxt),
            pl.BlockSpec(memory_space=pl.ANY),
            pl.BlockSpec((tc, d), lambda i: (i, 0)),
            pl.BlockSpec((None, 1, d), lambda i: (i * tc // seq, 0, 0)),
            pl.BlockSpec((tc, LANES), lambda i: (i, 0)),
        ],
        out_specs=pl.BlockSpec((tc, d), lambda i: (i, 0)),
        scratch_shapes=[pltpu.VMEM((2, 2 * tc, d), F32), pltpu.SemaphoreType.DMA((2,))],
        compiler_params=_params("arbitrary"),
        name="moe_combine",
    )(a3, b3, a3, b3, ys, x, gate.reshape(b, 1, d), meta_w)


def _moe_ffn_residual(x, hf, logits, gate, w_gate, w_up, w_down, seq):
    t, d = hf.shape
    n_exp = w_gate.shape[0]
    tm = _tile(t, TM_MOE)
    meta_i, meta_w, counts = _route(logits)
    e1, e2, p1, p2 = meta_i[:, 0], meta_i[:, 1], meta_i[:, 2], meta_i[:, 3]
    cnt = counts[0, :n_exp].astype(I32)
    padded = (cnt + tm - 1) // tm * tm
    ends = jnp.cumsum(padded)
    starts = ends - padded
    d1, d2 = starts[e1] + p1, starts[e2] + p2
    n_rows = 2 * t + n_exp * tm
    tok = jnp.arange(t, dtype=I32)
    src = jnp.zeros((n_rows,), I32).at[d1].set(tok).at[d2].set(tok)
    tile_start = jnp.arange(n_rows // tm, dtype=I32) * tm
    tile_expert = jnp.minimum(jnp.sum(tile_start[:, None] >= ends[None, :], axis=1), n_exp - 1).astype(I32)
    tile_valid = (tile_start < ends[-1]).astype(I32)
    tile_first = jnp.concatenate([jnp.ones((1,), I32), (tile_expert[1:] != tile_expert[:-1]).astype(I32)])
    sched = (tile_expert, tile_valid, tile_first)

    xs = _dispatch_rows(hf, src)
    hs = _ws_mm(_ws_glu_kernel, xs, (w_gate, w_up), sched, tm, TN_MOE, BF16, "moe_glu")
    ys = _ws_mm(_ws_mm_kernel, hs, (w_down,), sched, tm, TN_MOE, F32, "moe_down")
    return _combine_rows(ys, d1, d2, meta_w, x, gate, seq)


def kernel(x, c, ada_w, ada_b, norm_mix_g, norm_ffn_g, conv_w_in, conv_w, conv_w_out, nsa_w_in, nsa_w_o,
           nsa_q_norm_g, nsa_k_norm_g, cmp_pos, cmp_w1, cmp_w2, ffn_w_gate, ffn_w_up, ffn_w_down,
           router_w, router_b, moe_w_gate, moe_w_up, moe_w_down):
    batch, seq, d = x.shape
    t = batch * seq
    bf = lambda a: a.astype(BF16)

    mod = _ada_mod(c, ada_w, ada_b)
    sh_m, sc_m, g_m, sh_f, sc_f, g_f = [mod[:, :, i * d:(i + 1) * d] for i in range(6)]
    xt = x.reshape(t, d)

    hm = _norm_mod(xt, norm_mix_g[0], sc_m[0], sh_m[0], seq)
    bcu = _mm(hm, bf(conv_w_in[0]), BF16, "conv_in")
    z = _conv_gate(bcu, conv_w[0], batch, seq)
    xt = _mm_res(z, bf(conv_w_out[0]), xt, g_m[0], seq, "conv_out")
    hf = _norm_mod(xt, norm_ffn_g[0], sc_f[0], sh_f[0], seq)
    tm_ffn = _tile(t, TM)
    hid = _ws_mm(_ws_glu_kernel, hf, (ffn_w_gate[0:1], ffn_w_up[0:1]), _dense_sched(t // tm_ffn), tm_ffn,
                 TN_GLU, BF16, "ffn_glu")
    xt = _mm_res(hid, bf(ffn_w_down[0]), xt, g_f[0], seq, "ffn_down", tk=TK_DOWN)

    hm = _norm_mod(xt, norm_mix_g[1], sc_m[1], sh_m[1], seq)
    qkv_w = N_HEADS * HEAD_DIM + 6 * KV_WIDTH
    w_in = nsa_w_in[0]
    proj = _mm(hm, bf(w_in[:, :qkv_w]), F32, "nsa_in")
    w_gates = jnp.pad(w_in[:, qkv_w:], ((0, 0), (0, LANES - 3 * N_HEADS)))
    gates_pre = _mm(hm, bf(w_gates), F32, "nsa_gates")
    pos = jnp.arange(seq, dtype=I32)
    cos, sin = _rope_tables(pos)
    qr, ks, vs, kw, vw, gates = _nsa_prep(proj, gates_pre, cos, sin, nsa_q_norm_g[0], nsa_k_norm_g[0], seq)
    cmp_end = jnp.arange(seq // CMP_STRIDE, dtype=I32) * CMP_STRIDE + CMP_BLOCK - 1
    cos_c, sin_c = _rope_tables(cmp_end)
    kcmp, vcmp = _compress(proj.reshape(batch, seq, qkv_w), bf(cmp_w1[0]), bf(cmp_w2[0]), cmp_pos[0],
                           nsa_k_norm_g[0, 0:1], cos_c, sin_c)
    attn = _nsa_attn(qr, kcmp, vcmp, ks, vs, kw, vw, gates, batch, seq)
    xt = _mm_res(attn, bf(nsa_w_o[0]), xt, g_m[1], seq, "nsa_out")

    n_exp = router_w.shape[2]
    rw = jnp.pad(router_w[0], ((0, 0), (0, LANES - n_exp)))
    rb = jnp.pad(router_b[0], (0, LANES - n_exp)).reshape(1, LANES)
    hf, logits = _norm_mod(xt, norm_ffn_g[1], sc_f[1], sh_f[1], seq, router=(rw, rb))
    out = _moe_ffn_residual(xt, hf, logits, g_f[1], moe_w_gate[0], moe_w_up[0], moe_w_down[0], seq)
    return out.reshape(batch, seq, d)
```

```python
import functools

import jax
import jax.numpy as jnp
from jax import lax
from jax.experimental import pallas as pl
from jax.experimental.pallas import tpu as pltpu

F32 = jnp.float32
BF16 = jnp.bfloat16
I32 = jnp.int32

EPS = 1e-6
NEG_INF = -1e30
FORCE_BONUS = 1e4
ROPE_THETA = 10000.0
LOG2E = 1.4426950408889634

CONV_WIDTH = 3
N_HEADS = 32
HEAD_DIM = 128
N_KV_GROUPS = 4
HEADS_PER_GROUP = N_HEADS // N_KV_GROUPS
KV_WIDTH = N_KV_GROUPS * HEAD_DIM
CMP_BLOCK = 32
CMP_STRIDE = 16
SEL_BLOCK = 64
SEL_TOP_N = 16
WINDOW = 512
N_EXPERTS = 8

LANES = 128
VMEM_LIMIT_V7X = 56 * 1024 * 1024

TM = 1024
TN = 1024
TN_RES = 1024
TN_GLU = 256
TM_DOWN = 512
TN_DOWN = 512
TM_ROW = 512
TC_CONV = 512
TQ = 128
TK_SEL = 512
ATTN_ROW_BLOCK = 64
TM_MOE = 512
TN_MOE = 512
TN_MOE_DOWN = 1024
TG = 256
TCMB = 128


def _params(*sem):
    return pltpu.CompilerParams(dimension_semantics=sem, vmem_limit_bytes=VMEM_LIMIT_V7X)


def _tile(full, want):
    return want if full % want == 0 else full


def _ada_kernel(c_ref, w_ref, b_ref, o_ref):
    c = c_ref[...]
    cond = c * jax.nn.sigmoid(c)
    acc = jnp.dot(cond.astype(BF16), w_ref[...].astype(BF16), preferred_element_type=F32)
    o_ref[...] = acc + b_ref[...]


def _ada_mod(c, ada_w, ada_b):
    depth, d, n = ada_w.shape
    b = c.shape[0]
    tn = _tile(n, 512)
    return pl.pallas_call(
        _ada_kernel,
        out_shape=jax.ShapeDtypeStruct((depth, b, n), F32),
        grid=(depth, n // tn),
        in_specs=[
            pl.BlockSpec((b, d), lambda l, j: (0, 0)),
            pl.BlockSpec((None, d, tn), lambda l, j: (l, 0, j)),
            pl.BlockSpec((None, 1, tn), lambda l, j: (l, 0, j)),
        ],
        out_specs=pl.BlockSpec((None, b, tn), lambda l, j: (l, 0, j)),
        compiler_params=_params("parallel", "parallel"),
        name="ada_mod",
    )(c, ada_w, ada_b.reshape(depth, 1, n))


def _norm_mod_value(x_ref, g_ref, sc_ref, sh_ref):
    x = x_ref[...]
    ms = jnp.mean(x * x, axis=-1, keepdims=True)
    y = x * lax.rsqrt(ms + EPS) * g_ref[...]
    return y * (1.0 + sc_ref[...]) + sh_ref[...]


def _norm_mod_kernel(x_ref, g_ref, sc_ref, sh_ref, o_ref):
    o_ref[...] = _norm_mod_value(x_ref, g_ref, sc_ref, sh_ref).astype(o_ref.dtype)


def _norm_mod_router_kernel(x_ref, g_ref, sc_ref, sh_ref, rw_ref, rb_ref, o_ref, lg_ref):
    h = _norm_mod_value(x_ref, g_ref, sc_ref, sh_ref)
    o_ref[...] = h.astype(o_ref.dtype)
    lg_ref[...] = jnp.dot(h, rw_ref[...], preferred_element_type=F32,
                          precision=lax.Precision.HIGHEST) + rb_ref[...]


def _norm_mod(x, g, sc, sh, seq, router=None):
    t, d = x.shape
    b = sc.shape[0]
    tm = _tile(seq, TM_ROW)
    row_specs = [
        pl.BlockSpec((tm, d), lambda i: (i, 0)),
        pl.BlockSpec((1, d), lambda i: (0, 0)),
        pl.BlockSpec((None, 1, d), lambda i: (i * tm // seq, 0, 0)),
        pl.BlockSpec((None, 1, d), lambda i: (i * tm // seq, 0, 0)),
    ]
    args = [x, g.reshape(1, d), sc.reshape(b, 1, d), sh.reshape(b, 1, d)]
    if router is None:
        return pl.pallas_call(
            _norm_mod_kernel,
            out_shape=jax.ShapeDtypeStruct((t, d), BF16),
            grid=(t // tm,),
            in_specs=row_specs,
            out_specs=pl.BlockSpec((tm, d), lambda i: (i, 0)),
            compiler_params=_params("parallel"),
            name="norm_mod",
        )(*args)
    rw, rb = router
    return pl.pallas_call(
        _norm_mod_router_kernel,
        out_shape=(jax.ShapeDtypeStruct((t, d), F32), jax.ShapeDtypeStruct((t, LANES), F32)),
        grid=(t // tm,),
        in_specs=row_specs + [
            pl.BlockSpec((d, LANES), lambda i: (0, 0)),
            pl.BlockSpec((1, LANES), lambda i: (0, 0)),
        ],
        out_specs=(pl.BlockSpec((tm, d), lambda i: (i, 0)), pl.BlockSpec((tm, LANES), lambda i: (i, 0))),
        compiler_params=_params("parallel"),
        name="norm_mod_router",
    )(*args, rw, rb)


def _mm_kernel(x_ref, w_ref, o_ref):
    o_ref[...] = jnp.dot(x_ref[...], w_ref[...], preferred_element_type=F32).astype(o_ref.dtype)


def _mm(x, w, out_dtype, name, tm=TM, tn=TN):
    m, k = x.shape
    n = w.shape[1]
    tm, tn = _tile(m, tm), _tile(n, tn)
    return pl.pallas_call(
        _mm_kernel,
        out_shape=jax.ShapeDtypeStruct((m, n), out_dtype),
        grid=(m // tm, n // tn),
        in_specs=[pl.BlockSpec((tm, k), lambda i, j: (i, 0)), pl.BlockSpec((k, tn), lambda i, j: (0, j))],
        out_specs=pl.BlockSpec((tm, tn), lambda i, j: (i, j)),
        compiler_params=_params("parallel", "parallel"),
        name=name,
    )(x, w)


def _mm_res_kernel(x_ref, w_ref, res_ref, g_ref, o_ref):
    o_ref[...] = res_ref[...] + g_ref[...] * jnp.dot(x_ref[...], w_ref[...], preferred_element_type=F32)


def _mm_res(x, w, res, gate, seq, name, tm=TM, tn=TN_RES):
    m, k = x.shape
    n = w.shape[1]
    b = gate.shape[0]
    tm, tn = _tile(seq, tm), _tile(n, tn)
    return pl.pallas_call(
        _mm_res_kernel,
        out_shape=jax.ShapeDtypeStruct((m, n), F32),
        grid=(m // tm, n // tn),
        in_specs=[
            pl.BlockSpec((tm, k), lambda i, j: (i, 0)),
            pl.BlockSpec((k, tn), lambda i, j: (0, j)),
            pl.BlockSpec((tm, tn), lambda i, j: (i, j)),
            pl.BlockSpec((None, 1, tn), lambda i, j: (i * tm // seq, 0, j)),
        ],
        out_specs=pl.BlockSpec((tm, tn), lambda i, j: (i, j)),
        compiler_params=_params("parallel", "parallel"),
        name=name,
    )(x, w, res, gate.reshape(b, 1, n))


def _silu_mul(a, b):
    return (a * jax.nn.sigmoid(a)) * b


def _ws_glu_kernel(te_ref, tv_ref, tf_ref, x_ref, wg_ref, wu_ref, o_ref, wg_bf, wu_bf):
    i = pl.program_id(1)

    @pl.when(tf_ref[i] == 1)
    def _():
        wg_bf[...] = wg_ref[...].astype(BF16)
        wu_bf[...] = wu_ref[...].astype(BF16)

    @pl.when(tv_ref[i] == 1)
    def _():
        x = x_ref[...]
        a = jnp.dot(x, wg_bf[...], preferred_element_type=F32)
        b = jnp.dot(x, wu_bf[...], preferred_element_type=F32)
        o_ref[...] = _silu_mul(a, b).astype(o_ref.dtype)

    @pl.when(tv_ref[i] == 0)
    def _():
        o_ref[...] = jnp.zeros_like(o_ref)


def _ws_mm_kernel(te_ref, tv_ref, tf_ref, x_ref, w_ref, o_ref, w_bf):
    i = pl.program_id(1)

    @pl.when(tf_ref[i] == 1)
    def _():
        w_bf[...] = w_ref[...].astype(BF16)

    @pl.when(tv_ref[i] == 1)
    def _():
        o_ref[...] = jnp.dot(x_ref[...], w_bf[...], preferred_element_type=F32).astype(o_ref.dtype)

    @pl.when(tv_ref[i] == 0)
    def _():
        o_ref[...] = jnp.zeros_like(o_ref)


def _ws_mm(kernel, xs, ws, sched, tm, tn, out_dtype, name):
    r, k = xs.shape
    n = ws[0].shape[2]
    tn = _tile(n, tn)
    w_spec = pl.BlockSpec((None, k, tn), lambda j, i, te, tv, tf: (te[i], 0, j))
    return pl.pallas_call(
        kernel,
        out_shape=jax.ShapeDtypeStruct((r, n), out_dtype),
        grid_spec=pltpu.PrefetchScalarGridSpec(
            num_scalar_prefetch=3,
            grid=(n // tn, r // tm),
            in_specs=[pl.BlockSpec((tm, k), lambda j, i, te, tv, tf: (i, 0))] + [w_spec] * len(ws),
            out_specs=pl.BlockSpec((tm, tn), lambda j, i, te, tv, tf: (i, j)),
            scratch_shapes=[pltpu.VMEM((k, tn), BF16)] * len(ws),
        ),
        compiler_params=_params("parallel", "arbitrary"),
        name=name,
    )(*sched, xs, *ws)


def _dense_sched(n_tiles):
    first = (jnp.arange(n_tiles, dtype=I32) == 0).astype(I32)
    return jnp.zeros((n_tiles,), I32), jnp.ones((n_tiles,), I32), first


def _conv_gate_kernel(b_ref, c_ref, u_ref, w_ref, o_ref):
    v = c_ref[...].astype(F32) * u_ref[...].astype(F32)
    rows = lax.broadcasted_iota(I32, v.shape, 0)
    v1 = jnp.where(rows >= 1, pltpu.roll(v, 1, 0), 0.0)
    v2 = jnp.where(rows >= 2, pltpu.roll(v, 2, 0), 0.0)
    w = w_ref[...]
    y = w[0:1, :] * v2 + w[1:2, :] * v1 + w[2:3, :] * v
    o_ref[...] = (b_ref[...].astype(F32) * y).astype(o_ref.dtype)


def _conv_gate(bcu, conv_w, batch, seq):
    d = conv_w.shape[1]
    tc = _tile(d, TC_CONV)
    nc = d // tc
    bcu3 = bcu.reshape(batch, seq, 3 * d)
    out = pl.pallas_call(
        _conv_gate_kernel,
        out_shape=jax.ShapeDtypeStruct((batch, seq, d), BF16),
        grid=(batch, nc),
        in_specs=[
            pl.BlockSpec((None, seq, tc), lambda b, j: (b, 0, j)),
            pl.BlockSpec((None, seq, tc), lambda b, j: (b, 0, nc + j)),
            pl.BlockSpec((None, seq, tc), lambda b, j: (b, 0, 2 * nc + j)),
            pl.BlockSpec((CONV_WIDTH, tc), lambda b, j: (0, j)),
        ],
        out_specs=pl.BlockSpec((None, seq, tc), lambda b, j: (b, 0, j)),
        compiler_params=_params("parallel", "parallel"),
        name="conv_gate",
    )(bcu3, bcu3, bcu3, conv_w)
    return out.reshape(batch * seq, d)


def _rope_tables(pos):
    half = HEAD_DIM // 2
    inv = ROPE_THETA ** (-jnp.arange(half, dtype=F32) / half)
    ang = pos.astype(F32)[:, None] * inv[None, :]
    cos, sin = jnp.cos(ang), jnp.sin(ang)
    return jnp.concatenate([cos, cos], axis=-1), jnp.concatenate([-sin, sin], axis=-1)


def _norm_rope(x, g, cos, sin_signed):
    ms = jnp.mean(x * x, axis=-1, keepdims=True)
    xn = x * lax.rsqrt(ms + EPS) * g
    return xn * cos + pltpu.roll(xn, HEAD_DIM // 2, 1) * sin_signed


def _nsa_prep_kernel(p_ref, gp_ref, cos_ref, sin_ref, qg_ref, kg_ref,
                     q_o, ks_o, vs_o, kw_o, vw_o, g_o, *, scale):
    cos, sin = cos_ref[...], sin_ref[...]
    hd = HEAD_DIM
    qg = qg_ref[...]
    for h in range(N_HEADS):
        x = p_ref[:, h * hd:(h + 1) * hd]
        q_o[:, h * hd:(h + 1) * hd] = (_norm_rope(x, qg, cos, sin) * scale).astype(q_o.dtype)
    kv0 = N_HEADS * hd
    for g in range(N_KV_GROUPS):
        def col(i, g=g):
            c0 = kv0 + i * KV_WIDTH + g * hd
            return p_ref[:, c0:c0 + hd]
        sl = slice(g * hd, (g + 1) * hd)
        ks_o[:, sl] = _norm_rope(col(2), kg_ref[1:2, :], cos, sin).astype(ks_o.dtype)
        vs_o[:, sl] = col(3).astype(vs_o.dtype)
        kw_o[:, sl] = _norm_rope(col(4), kg_ref[2:3, :], cos, sin).astype(kw_o.dtype)
        vw_o[:, sl] = col(5).astype(vw_o.dtype)
    sig = jax.nn.sigmoid(gp_ref[...])
    per_group = 3 * HEADS_PER_GROUP
    for g in range(N_KV_GROUPS):
        shift = (LANES - g * per_group) % LANES
        g_o[:, g * LANES:(g + 1) * LANES] = sig if shift == 0 else pltpu.roll(sig, shift, 1)


def _nsa_prep(proj, gates_pre, cos, sin, q_g, k_g, seq):
    t = proj.shape[0]
    tm = _tile(seq, 256)
    nseq = seq // tm
    qd = N_HEADS * HEAD_DIM
    kvspec = pl.BlockSpec((tm, KV_WIDTH), lambda i: (i, 0))
    kvshape = jax.ShapeDtypeStruct((t, KV_WIDTH), BF16)
    return pl.pallas_call(
        functools.partial(_nsa_prep_kernel, scale=HEAD_DIM ** -0.5 * LOG2E),
        out_shape=(jax.ShapeDtypeStruct((t, qd), BF16), kvshape, kvshape, kvshape, kvshape,
                   jax.ShapeDtypeStruct((t, N_KV_GROUPS * LANES), F32)),
        grid=(t // tm,),
        in_specs=[
            pl.BlockSpec((tm, proj.shape[1]), lambda i: (i, 0)),
            pl.BlockSpec((tm, LANES), lambda i: (i, 0)),
            pl.BlockSpec((tm, HEAD_DIM), lambda i: (i % nseq, 0)),
            pl.BlockSpec((tm, HEAD_DIM), lambda i: (i % nseq, 0)),
            pl.BlockSpec((1, HEAD_DIM), lambda i: (0, 0)),
            pl.BlockSpec((3, HEAD_DIM), lambda i: (0, 0)),
        ],
        out_specs=(pl.BlockSpec((tm, qd), lambda i: (i, 0)), kvspec, kvspec, kvspec, kvspec,
                   pl.BlockSpec((tm, N_KV_GROUPS * LANES), lambda i: (i, 0))),
        compiler_params=_params("parallel"),
        name="nsa_prep",
    )(proj, gates_pre, cos, sin, q_g.reshape(1, HEAD_DIM), k_g)


def _compress_kernel(kc_ref, vc_ref, w1_ref, w2_ref, pos_ref, kg_ref, cos_ref, sin_ref,
                     ko_ref, vo_ref, *, nchunk):
    def comp(x_ref, i):
        first = jnp.zeros((nchunk, w1_ref.shape[-1]), F32)
        second = jnp.zeros_like(first)
        for l in range(CMP_STRIDE):
            xl = x_ref[pl.ds(l, nchunk, stride=CMP_STRIDE), :]
            xa = (xl + pos_ref[i, l:l + 1, :]).astype(BF16)
            xb = (xl + pos_ref[i, CMP_STRIDE + l:CMP_STRIDE + l + 1, :]).astype(BF16)
            first += jnp.dot(xa, w1_ref[i, l], preferred_element_type=F32)
            second += jnp.dot(xb, w1_ref[i, CMP_STRIDE + l], preferred_element_type=F32)
        pre = first + pltpu.roll(second, nchunk - 1, 0)
        hid = pre * jax.nn.sigmoid(pre)
        return jnp.dot(hid.astype(BF16), w2_ref[i], preferred_element_type=F32)

    k = comp(kc_ref, 0)
    ko_ref[...] = _norm_rope(k, kg_ref[...], cos_ref[...], sin_ref[...]).astype(ko_ref.dtype)
    vo_ref[...] = comp(vc_ref, 1).astype(vo_ref.dtype)


def _compress(proj3, w1, w2, cmp_pos, kg0, cos_c, sin_c):
    batch, seq, _ = proj3.shape
    nchunk = seq // CMP_STRIDE
    hd = HEAD_DIM
    cb = N_HEADS
    out = jax.ShapeDtypeStruct((batch, N_KV_GROUPS, nchunk, hd), BF16)
    ospec = pl.BlockSpec((None, None, nchunk, hd), lambda b, g: (b, g, 0, 0))
    full = lambda a: pl.BlockSpec(a.shape, lambda b, g: (0,) * a.ndim)
    return pl.pallas_call(
        functools.partial(_compress_kernel, nchunk=nchunk),
        out_shape=(out, out),
        grid=(batch, N_KV_GROUPS),
        in_specs=[
            pl.BlockSpec((None, seq, hd), lambda b, g: (b, 0, cb + g)),
            pl.BlockSpec((None, seq, hd), lambda b, g: (b, 0, cb + N_KV_GROUPS + g)),
            full(w1), full(w2), full(cmp_pos), full(kg0), full(cos_c), full(sin_c),
        ],
        out_specs=(ospec, ospec),
        compiler_params=_params("parallel", "parallel"),
        name="nsa_compress",
    )(proj3, proj3, w1, w2, cmp_pos, kg0, cos_c, sin_c)


def _nsa_attn_kernel(q_ref, kc_ref, vc_ref, ks_ref, vs_ref, kw_ref, vw_ref, g_ref, ov_ref, e_ref,
                     o_ref, p_ref, b_ref, m_ref, l_ref, a_ref, acc_ref,
                     *, tq, tk, seq, n_sel, n_top, span):
    J = HEADS_PER_GROUP
    hd = HEAD_DIM
    q0 = pl.program_id(2) * tq
    q = q_ref[...]
    qa = jnp.concatenate([q[:, j * hd:(j + 1) * hd] for j in range(J)], axis=0)
    tpos = q0 + lax.broadcasted_iota(I32, (tq, 1), 0)

    def scores(kblk):
        return lax.dot_general(qa, kblk, (((1,), (1,)), ((), ())), preferred_element_type=F32)

    ncp = kc_ref.shape[0]
    n_idx = lax.broadcasted_iota(I32, (1, ncp), 1)
    mask_c = (n_idx * CMP_STRIDE + (CMP_BLOCK - 1)) <= tpos
    s3 = jnp.where(mask_c[None], scores(kc_ref[...]).reshape(J, tq, ncp), NEG_INF)
    m = jnp.max(s3, axis=-1, keepdims=True)
    p = jnp.where(mask_c[None], jnp.exp2(s3 - m), 0.0)
    l = jnp.sum(p, axis=-1, keepdims=True)
    pc = p * jnp.where(l > 0.0, 1.0 / l, 0.0)
    o_c = jnp.dot(pc.reshape(J * tq, ncp).astype(BF16), vc_ref[...], preferred_element_type=F32)

    nsp = e_ref.shape[1]
    imp = lax.dot_general(ov_ref[...], jnp.sum(pc, axis=0), (((1,), (1,)), ((), ())),
                          preferred_element_type=F32, precision=lax.Precision.HIGHEST)[:nsp]
    blk = lax.broadcasted_iota(I32, (nsp, 1), 0)
    tpos_l = q0 + lax.broadcasted_iota(I32, (1, tq), 1)
    cur = tpos_l // SEL_BLOCK
    forced = (blk == 0) | (blk == cur) | (blk == cur - 1)
    imp = jnp.where(blk * SEL_BLOCK <= tpos_l, imp + FORCE_BONUS * forced.astype(F32), NEG_INF)
    rank = jnp.zeros((nsp, tq), F32)
    for mp in range(n_sel):
        row = imp[mp:mp + 1, :]
        rank += jnp.where(blk > mp, (row >= imp).astype(F32), (row > imp).astype(F32))
    sel = jnp.transpose(((rank < n_top) & (blk < n_sel)).astype(F32)).astype(BF16)

    rb = ATTN_ROW_BLOCK

    def softmax_head(j, kblk, width, online):
        s_head = lax.dot_general(q[:, j * hd:(j + 1) * hd], kblk, (((1,), (1,)), ((), ())),
                                 preferred_element_type=F32)
        lane_tiles = [slice(c, c + LANES) for c in range(0, width, LANES)]
        for h0 in range(0, tq, rb):
            rows = slice(j * tq + h0, j * tq + h0 + rb)
            s = [s_head[h0:h0 + rb, c] + b_ref[h0:h0 + rb, c] for c in lane_tiles]
            m_new = jnp.broadcast_to(jnp.max(functools.reduce(jnp.maximum, s), axis=-1, keepdims=True),
                                     (rb, LANES))
            if online:
                m_old = m_ref[rows, :]
                m_new = jnp.maximum(m_old, m_new)
                alpha = jnp.exp2(m_old - m_new)
                m_ref[rows, :] = m_new
                a_ref[rows, :] = alpha
            p = [jnp.exp2(x - m_new) for x in s]
            row_sum = jnp.broadcast_to(jnp.sum(functools.reduce(jnp.add, p), axis=-1, keepdims=True),
                                       (rb, LANES))
            l_ref[rows, :] = alpha * l_ref[rows, :] + row_sum if online else row_sum
            for c, x in zip(lane_tiles, p):
                p_ref[rows, c] = x.astype(BF16)

    m_ref[...] = jnp.full(m_ref.shape, NEG_INF, F32)
    l_ref[...] = jnp.zeros(l_ref.shape, F32)
    acc_ref[...] = jnp.zeros(acc_ref.shape, F32)

    def sel_chunk(c, carry):
        k0 = pl.multiple_of(c * tk, tk)
        picked = jnp.dot(sel, e_ref[c], preferred_element_type=F32)
        kpos = k0 + lax.broadcasted_iota(I32, (1, tk), 1)
        b_ref[:, :tk] = jnp.where((picked > 0.5) & (kpos <= tpos), 0.0, NEG_INF)
        kblk = ks_ref[pl.ds(k0, tk), :]
        for j in range(J):
            softmax_head(j, kblk, tk, online=True)
        pv = jnp.dot(p_ref[:, :tk], vs_ref[pl.ds(k0, tk), :], preferred_element_type=F32)
        acc_ref[...] = a_ref[...] * acc_ref[...] + pv
        return carry

    lax.fori_loop(0, (q0 + tq + tk - 1) // tk, sel_chunk, 0)
    inv_ls = 1.0 / l_ref[...]

    start = pl.multiple_of(jnp.maximum(q0 + tq - span, 0), tq)
    diff = tpos - (start + lax.broadcasted_iota(I32, (1, span), 1))
    b_ref[:, :span] = jnp.where((diff >= 0) & (diff < WINDOW), 0.0, NEG_INF)
    kblk = kw_ref[pl.ds(start, span), :]
    for j in range(J):
        softmax_head(j, kblk, span, online=False)
    o_w = jnp.dot(p_ref[:, :span], vw_ref[pl.ds(start, span), :], preferred_element_type=F32)
    inv_lw = 1.0 / l_ref[...]

    gt = g_ref[...]
    for j in range(J):
        rows = slice(j * tq, (j + 1) * tq)
        o = (gt[:, 3 * j:3 * j + 1] * o_c[rows]
             + (gt[:, 3 * j + 1:3 * j + 2] * inv_ls[rows]) * acc_ref[rows, :]
             + (gt[:, 3 * j + 2:3 * j + 3] * inv_lw[rows]) * o_w[rows])
        o_ref[:, j * hd:(j + 1) * hd] = o.astype(o_ref.dtype)


def _nsa_attn(qr, kcmp, vcmp, ks, vs, kw, vw, gates, batch, seq):
    hd = HEAD_DIM
    J = HEADS_PER_GROUP
    tq = _tile(seq, TQ)
    tk = _tile(seq, TK_SEL)
    n_sel = seq // SEL_BLOCK
    n_top = min(SEL_TOP_N, n_sel)
    span = min(WINDOW + tq, seq)
    width = max(span, tk)
    ncp = kcmp.shape[2]
    nsp = -(-n_sel // 16) * 16
    ci = jnp.arange(ncp, dtype=I32)[None, :] * CMP_STRIDE
    sj = jnp.arange(LANES, dtype=I32)[:, None] * SEL_BLOCK
    ov = jnp.clip(jnp.minimum(ci + CMP_BLOCK, sj + SEL_BLOCK) - jnp.maximum(ci, sj), 0).astype(F32) / CMP_BLOCK
    ov = jnp.where((jnp.arange(ncp)[None, :] < ncp - 1) & (jnp.arange(LANES)[:, None] < n_sel), ov, 0.0)
    kk = jnp.arange(seq, dtype=I32).reshape(seq // tk, 1, tk)
    expand = (kk // SEL_BLOCK == jnp.arange(nsp, dtype=I32)[None, :, None]).astype(BF16)

    r3 = lambda a: a.reshape(batch, seq, a.shape[-1])
    kv_spec = pl.BlockSpec((None, seq, hd), lambda b, g, i: (b, 0, g))
    cmp_spec = pl.BlockSpec((None, None, ncp, hd), lambda b, g, i: (b, g, 0, 0))
    out = pl.pallas_call(
        functools.partial(_nsa_attn_kernel, tq=tq, tk=tk, seq=seq, n_sel=n_sel, n_top=n_top, span=span),
        out_shape=jax.ShapeDtypeStruct((batch, seq, N_HEADS * hd), BF16),
        grid=(batch, N_KV_GROUPS, seq // tq),
        in_specs=[
            pl.BlockSpec((None, tq, J * hd), lambda b, g, i: (b, i, g)),
            cmp_spec, cmp_spec, kv_spec, kv_spec, kv_spec, kv_spec,
            pl.BlockSpec((None, tq, LANES), lambda b, g, i: (b, i, g)),
            pl.BlockSpec(ov.shape, lambda b, g, i: (0, 0)),
            pl.BlockSpec(expand.shape, lambda b, g, i: (0, 0, 0)),
        ],
        out_specs=pl.BlockSpec((None, tq, J * hd), lambda b, g, i: (b, i, g)),
        scratch_shapes=[
            pltpu.VMEM((J * tq, width), BF16),
            pltpu.VMEM((tq, width), F32),
            pltpu.VMEM((J * tq, LANES), F32),
            pltpu.VMEM((J * tq, LANES), F32),
            pltpu.VMEM((J * tq, LANES), F32),
            pltpu.VMEM((J * tq, hd), F32),
        ],
        compiler_params=_params("parallel", "parallel", "parallel"),
        name="nsa_attn",
    )(r3(qr), kcmp, vcmp, r3(ks), r3(vs), r3(kw), r3(vw), r3(gates), ov, expand)
    return out.reshape(batch * seq, N_HEADS * hd)


def _route_kernel(lg_ref, tri_ref, mi_ref, mw_ref, cnt_ref, carry_ref):
    @pl.when(pl.program_id(0) == 0)
    def _():
        carry_ref[...] = jnp.zeros_like(carry_ref)

    lane = lax.broadcasted_iota(I32, lg_ref.shape, 1)
    lg = jnp.where(lane < N_EXPERTS, lg_ref[...], -jnp.inf)
    m1 = jnp.max(lg, axis=-1, keepdims=True)
    i1 = jnp.min(jnp.where(lg == m1, lane, LANES), axis=-1, keepdims=True)
    lg2 = jnp.where(lane == i1, -jnp.inf, lg)
    m2 = jnp.max(lg2, axis=-1, keepdims=True)
    i2 = jnp.min(jnp.where(lg2 == m2, lane, LANES), axis=-1, keepdims=True)
    e2 = jnp.exp(m2 - m1)
    w1 = 1.0 / (1.0 + e2)
    w2 = e2 / (1.0 + e2)
    oh1, oh2 = lane == i1, lane == i2
    onehot = (oh1 | oh2).astype(F32)
    before = jnp.dot(tri_ref[...], onehot.astype(BF16), preferred_element_type=F32) + carry_ref[...]
    pos1 = jnp.sum(jnp.where(oh1, before, 0.0), axis=-1, keepdims=True).astype(I32)
    pos2 = jnp.sum(jnp.where(oh2, before, 0.0), axis=-1, keepdims=True).astype(I32)
    carry_ref[...] += jnp.sum(onehot, axis=0, keepdims=True)
    mi_ref[...] = jnp.where(lane == 0, i1, jnp.where(lane == 1, i2,
                            jnp.where(lane == 2, pos1, jnp.where(lane == 3, pos2, 0))))
    mw_ref[...] = jnp.where(lane == 0, w1, jnp.where(lane == 1, w2, 0.0))
    cnt_ref[...] = carry_ref[...]


def _route(logits):
    t = logits.shape[0]
    tm = _tile(t, 512)
    tri = (jnp.arange(tm)[None, :] < jnp.arange(tm)[:, None]).astype(BF16)
    return pl.pallas_call(
        _route_kernel,
        out_shape=(jax.ShapeDtypeStruct((t, LANES), I32), jax.ShapeDtypeStruct((t, LANES), F32),
                   jax.ShapeDtypeStruct((1, LANES), F32)),
        grid=(t // tm,),
        in_specs=[pl.BlockSpec((tm, LANES), lambda i: (i, 0)), pl.BlockSpec((tm, tm), lambda i: (0, 0))],
        out_specs=(pl.BlockSpec((tm, LANES), lambda i: (i, 0)), pl.BlockSpec((tm, LANES), lambda i: (i, 0)),
                   pl.BlockSpec((1, LANES), lambda i: (0, 0))),
        scratch_shapes=[pltpu.VMEM((1, LANES), F32)],
        compiler_params=_params("arbitrary"),
        name="moe_route",
    )(logits, tri)


def _row_copy(src_hbm, row, buf, slot, r, sem):
    return pltpu.make_async_copy(src_hbm.at[pl.ds(row, 1), :], buf.at[slot, pl.ds(r, 1), :], sem.at[slot])


def _issue_rows(idx_ref, src_hbm, buf, slot, base, sem, rows):
    def body(r, carry):
        _row_copy(src_hbm, idx_ref[0, r], buf, slot, base + r, sem).start()
        return carry

    lax.fori_loop(0, rows, body, 0, unroll=8)


def _wait_rows(src_hbm, buf, slot, sem, rows):
    def body(r, carry):
        _row_copy(src_hbm, 0, buf, slot, r, sem).wait()
        return carry

    lax.fori_loop(0, rows, body, 0, unroll=8)


def _dispatch_kernel(idx_ref, nxt_ref, src_hbm, o_ref, buf, sem, *, rows):
    i = pl.program_id(0)
    slot = i % 2

    @pl.when(i == 0)
    def _():
        _issue_rows(idx_ref, src_hbm, buf, 0, 0, sem, rows)

    @pl.when(i + 1 < pl.num_programs(0))
    def _():
        _issue_rows(nxt_ref, src_hbm, buf, 1 - slot, 0, sem, rows)

    _wait_rows(src_hbm, buf, slot, sem, rows)
    o_ref[...] = buf[slot].astype(o_ref.dtype)


def _dispatch_rows(src, idx):
    n = idx.shape[0]
    d = src.shape[1]
    tg = _tile(n, TG)
    steps = n // tg
    smem = lambda f: pl.BlockSpec((None, 1, tg), f, memory_space=pltpu.SMEM)
    idx3 = idx.reshape(steps, 1, tg)
    return pl.pallas_call(
        functools.partial(_dispatch_kernel, rows=tg),
        out_shape=jax.ShapeDtypeStruct((n, d), BF16),
        grid=(steps,),
        in_specs=[
            smem(lambda i: (i, 0, 0)),
            smem(lambda i: (jnp.minimum(i + 1, steps - 1), 0, 0)),
            pl.BlockSpec(memory_space=pl.ANY),
        ],
        out_specs=pl.BlockSpec((tg, d), lambda i: (i, 0)),
        scratch_shapes=[pltpu.VMEM((2, tg, d), F32), pltpu.SemaphoreType.DMA((2,))],
        compiler_params=_params("arbitrary"),
        name="moe_dispatch",
    )(idx3, idx3, src)


def _combine_kernel(a_ref, b_ref, na_ref, nb_ref, y_hbm, x_ref, g_ref, mw_ref, o_ref, buf, sem, *, rows):
    i = pl.program_id(0)
    slot = i % 2

    def issue(first_ref, second_ref, s):
        _issue_rows(first_ref, y_hbm, buf, s, 0, sem, rows)
        _issue_rows(second_ref, y_hbm, buf, s, rows, sem, rows)

    @pl.when(i == 0)
    def _():
        issue(a_ref, b_ref, 0)

    @pl.when(i + 1 < pl.num_programs(0))
    def _():
        issue(na_ref, nb_ref, 1 - slot)

    _wait_rows(y_hbm, buf, slot, sem, 2 * rows)
    mw = mw_ref[...]
    f = mw[:, 0:1] * buf[slot, 0:rows, :] + mw[:, 1:2] * buf[slot, rows:2 * rows, :]
    o_ref[...] = x_ref[...] + g_ref[...] * f


def _combine_rows(ys, d1, d2, meta_w, x, gate, seq):
    t, d = x.shape
    b = gate.shape[0]
    tc = _tile(seq, TCMB)
    steps = t // tc
    smem = lambda f: pl.BlockSpec((None, 1, tc), f, memory_space=pltpu.SMEM)
    cur, nxt = (lambda i: (i, 0, 0)), (lambda i: (jnp.minimum(i + 1, steps - 1), 0, 0))
    a3, b3 = d1.reshape(steps, 1, tc), d2.reshape(steps, 1, tc)
    return pl.pallas_call(
        functools.partial(_combine_kernel, rows=tc),
        out_shape=jax.ShapeDtypeStruct((t, d), F32),
        grid=(steps,),
        in_specs=[
            smem(cur), smem(cur), smem(nxt), smem(nxt),
            pl.BlockSpec(memory_space=pl.ANY),
            pl.BlockSpec((tc, d), lambda i: (i, 0)),
            pl.BlockSpec((None, 1, d), lambda i: (i * tc // seq, 0, 0)),
            pl.BlockSpec((tc, LANES), lambda i: (i, 0)),
        ],
        out_specs=pl.BlockSpec((tc, d), lambda i: (i, 0)),
        scratch_shapes=[pltpu.VMEM((2, 2 * tc, d), F32), pltpu.SemaphoreType.DMA((2,))],
        compiler_params=_params("arbitrary"),
        name="moe_combine",
    )(a3, b3, a3, b3, ys, x, gate.reshape(b, 1, d), meta_w)


def _moe_ffn_residual(x, hf, logits, gate, w_gate, w_up, w_down, seq):
    t, d = hf.shape
    n_exp = w_gate.shape[0]
    tm = _tile(t, TM_MOE)
    meta_i, meta_w, counts = _route(logits)
    e1, e2, p1, p2 = meta_i[:, 0], meta_i[:, 1], meta_i[:, 2], meta_i[:, 3]
    cnt = counts[0, :n_exp].astype(I32)
    padded = (cnt + tm - 1) // tm * tm
    ends = jnp.cumsum(padded)
    starts = ends - padded
    d1, d2 = starts[e1] + p1, starts[e2] + p2
    n_rows = 2 * t + n_exp * tm
    tok = jnp.arange(t, dtype=I32)
    src = jnp.zeros((n_rows,), I32).at[d1].set(tok).at[d2].set(tok)
    tile_start = jnp.arange(n_rows // tm, dtype=I32) * tm
    tile_expert = jnp.minimum(jnp.sum(tile_start[:, None] >= ends[None, :], axis=1), n_exp - 1).astype(I32)
    tile_valid = (tile_start < ends[-1]).astype(I32)
    tile_first = jnp.concatenate([jnp.ones((1,), I32), (tile_expert[1:] != tile_expert[:-1]).astype(I32)])
    sched = (tile_expert, tile_valid, tile_first)

    xs = _dispatch_rows(hf, src)
    hs = _ws_mm(_ws_glu_kernel, xs, (w_gate, w_up), sched, tm, TN_MOE, BF16, "moe_glu")
    ys = _ws_mm(_ws_mm_kernel, hs, (w_down,), sched, tm, TN_MOE_DOWN, F32, "moe_down")
    return _combine_rows(ys, d1, d2, meta_w, x, gate, seq)


def kernel(x, c, ada_w, ada_b, norm_mix_g, norm_ffn_g, conv_w_in, conv_w, conv_w_out, nsa_w_in, nsa_w_o,
           nsa_q_norm_g, nsa_k_norm_g, cmp_pos, cmp_w1, cmp_w2, ffn_w_gate, ffn_w_up, ffn_w_down,
           router_w, router_b, moe_w_gate, moe_w_up, moe_w_down):
    batch, seq, d = x.shape
    t = batch * seq
    bf = lambda a: a.astype(BF16)

    mod = _ada_mod(c, ada_w, ada_b)
    sh_m, sc_m, g_m, sh_f, sc_f, g_f = [mod[:, :, i * d:(i + 1) * d] for i in range(6)]
    xt = x.reshape(t, d)

    hm = _norm_mod(xt, norm_mix_g[0], sc_m[0], sh_m[0], seq)
    bcu = _mm(hm, bf(conv_w_in[0]), BF16, "conv_in")
    z = _conv_gate(bcu, conv_w[0], batch, seq)
    xt = _mm_res(z, bf(conv_w_out[0]), xt, g_m[0], seq, "conv_out")
    hf = _norm_mod(xt, norm_ffn_g[0], sc_f[0], sh_f[0], seq)
    tm_ffn = _tile(t, TM)
    hid = _ws_mm(_ws_glu_kernel, hf, (ffn_w_gate[0:1], ffn_w_up[0:1]), _dense_sched(t // tm_ffn), tm_ffn,
                 TN_GLU, BF16, "ffn_glu")
    xt = _mm_res(hid, bf(ffn_w_down[0]), xt, g_f[0], seq, "ffn_down", tm=TM_DOWN, tn=TN_DOWN)

    hm = _norm_mod(xt, norm_mix_g[1], sc_m[1], sh_m[1], seq)
    qkv_w = N_HEADS * HEAD_DIM + 6 * KV_WIDTH
    w_in = nsa_w_in[0]
    proj = _mm(hm, bf(w_in[:, :qkv_w]), F32, "nsa_in")
    w_gates = jnp.pad(w_in[:, qkv_w:], ((0, 0), (0, LANES - 3 * N_HEADS)))
    gates_pre = _mm(hm, bf(w_gates), F32, "nsa_gates")
    pos = jnp.arange(seq, dtype=I32)
    cos, sin = _rope_tables(pos)
    qr, ks, vs, kw, vw, gates = _nsa_prep(proj, gates_pre, cos, sin, nsa_q_norm_g[0], nsa_k_norm_g[0], seq)
    cmp_end = jnp.arange(seq // CMP_STRIDE, dtype=I32) * CMP_STRIDE + CMP_BLOCK - 1
    cos_c, sin_c = _rope_tables(cmp_end)
    kcmp, vcmp = _compress(proj.reshape(batch, seq, qkv_w), bf(cmp_w1[0]), bf(cmp_w2[0]), cmp_pos[0],
                           nsa_k_norm_g[0, 0:1], cos_c, sin_c)
    attn = _nsa_attn(qr, kcmp, vcmp, ks, vs, kw, vw, gates, batch, seq)
    xt = _mm_res(attn, bf(nsa_w_o[0]), xt, g_m[1], seq, "nsa_out")

    n_exp = router_w.shape[2]
    rw = jnp.pad(router_w[0], ((0, 0), (0, LANES - n_exp)))
    rb = jnp.pad(router_b[0], (0, LANES - n_exp)).reshape(1, LANES)
    hf, logits = _norm_mod(xt, norm_ffn_g[1], sc_f[1], sh_f[1], seq, router=(rw, rb))
    out = _moe_ffn_residual(xt, hf, logits, g_f[1], moe_w_gate[0], moe_w_up[0], moe_w_down[0], seq)
    return out.reshape(batch, seq, d)
```

```python
import functools

import jax
import jax.numpy as jnp
from jax import lax
from jax.experimental import pallas as pl
from jax.experimental.pallas import tpu as pltpu

F32 = jnp.float32
BF16 = jnp.bfloat16
I32 = jnp.int32

EPS = 1e-6
NEG_INF = -1e30
FORCE_BONUS = 1e4
ROPE_THETA = 10000.0
LOG2E = 1.4426950408889634

CONV_WIDTH = 3
N_HEADS = 32
HEAD_DIM = 128
N_KV_GROUPS = 4
HEADS_PER_GROUP = N_HEADS // N_KV_GROUPS
KV_WIDTH = N_KV_GROUPS * HEAD_DIM
CMP_BLOCK = 32
CMP_STRIDE = 16
SEL_BLOCK = 64
SEL_TOP_N = 16
WINDOW = 512
N_EXPERTS = 8

LANES = 128
VMEM_LIMIT_V7X = 60 * 1024 * 1024

TM = 1024
TN = 1024
TN_RES = 1024
TN_GLU = 256
TM_DOWN = 512
TN_DOWN = 512
TM_ROW = 512
TC_CONV = 512
TQ = 256
TK_SEL = 512
ATTN_ROW_BLOCK = 64
TM_MOE = 512
TN_MOE = 512
TN_MOE_DOWN = 1024
TG = 256
TCMB = 128


def _params(*sem):
    return pltpu.CompilerParams(dimension_semantics=sem, vmem_limit_bytes=VMEM_LIMIT_V7X)


def _tile(full, want):
    return want if full % want == 0 else full


def _ada_kernel(c_ref, w_ref, b_ref, o_ref):
    c = c_ref[...]
    cond = c * jax.nn.sigmoid(c)
    acc = jnp.dot(cond.astype(BF16), w_ref[...].astype(BF16), preferred_element_type=F32)
    o_ref[...] = acc + b_ref[...]


def _ada_mod(c, ada_w, ada_b):
    depth, d, n = ada_w.shape
    b = c.shape[0]
    tn = _tile(n, 512)
    return pl.pallas_call(
        _ada_kernel,
        out_shape=jax.ShapeDtypeStruct((depth, b, n), F32),
        grid=(depth, n // tn),
        in_specs=[
            pl.BlockSpec((b, d), lambda l, j: (0, 0)),
            pl.BlockSpec((None, d, tn), lambda l, j: (l, 0, j)),
            pl.BlockSpec((None, 1, tn), lambda l, j: (l, 0, j)),
        ],
        out_specs=pl.BlockSpec((None, b, tn), lambda l, j: (l, 0, j)),
        compiler_params=_params("parallel", "parallel"),
        name="ada_mod",
    )(c, ada_w, ada_b.reshape(depth, 1, n))


def _norm_mod_value(x_ref, g_ref, sc_ref, sh_ref):
    x = x_ref[...]
    ms = jnp.mean(x * x, axis=-1, keepdims=True)
    y = x * lax.rsqrt(ms + EPS) * g_ref[...]
    return y * (1.0 + sc_ref[...]) + sh_ref[...]


def _norm_mod_kernel(x_ref, g_ref, sc_ref, sh_ref, o_ref):
    o_ref[...] = _norm_mod_value(x_ref, g_ref, sc_ref, sh_ref).astype(o_ref.dtype)


def _norm_mod_router_kernel(x_ref, g_ref, sc_ref, sh_ref, rw_ref, rb_ref, o_ref, lg_ref):
    h = _norm_mod_value(x_ref, g_ref, sc_ref, sh_ref)
    o_ref[...] = h.astype(o_ref.dtype)
    lg_ref[...] = jnp.dot(h, rw_ref[...], preferred_element_type=F32,
                          precision=lax.Precision.HIGHEST) + rb_ref[...]


def _norm_mod(x, g, sc, sh, seq, router=None):
    t, d = x.shape
    b = sc.shape[0]
    tm = _tile(seq, TM_ROW)
    row_specs = [
        pl.BlockSpec((tm, d), lambda i: (i, 0)),
        pl.BlockSpec((1, d), lambda i: (0, 0)),
        pl.BlockSpec((None, 1, d), lambda i: (i * tm // seq, 0, 0)),
        pl.BlockSpec((None, 1, d), lambda i: (i * tm // seq, 0, 0)),
    ]
    args = [x, g.reshape(1, d), sc.reshape(b, 1, d), sh.reshape(b, 1, d)]
    if router is None:
        return pl.pallas_call(
            _norm_mod_kernel,
            out_shape=jax.ShapeDtypeStruct((t, d), BF16),
            grid=(t // tm,),
            in_specs=row_specs,
            out_specs=pl.BlockSpec((tm, d), lambda i: (i, 0)),
            compiler_params=_params("parallel"),
            name="norm_mod",
        )(*args)
    rw, rb = router
    return pl.pallas_call(
        _norm_mod_router_kernel,
        out_shape=(jax.ShapeDtypeStruct((t, d), F32), jax.ShapeDtypeStruct((t, LANES), F32)),
        grid=(t // tm,),
        in_specs=row_specs + [
            pl.BlockSpec((d, LANES), lambda i: (0, 0)),
            pl.BlockSpec((1, LANES), lambda i: (0, 0)),
        ],
        out_specs=(pl.BlockSpec((tm, d), lambda i: (i, 0)), pl.BlockSpec((tm, LANES), lambda i: (i, 0))),
        compiler_params=_params("parallel"),
        name="norm_mod_router",
    )(*args, rw, rb)


def _mm_kernel(x_ref, w_ref, o_ref):
    o_ref[...] = jnp.dot(x_ref[...], w_ref[...], preferred_element_type=F32).astype(o_ref.dtype)


def _mm(x, w, out_dtype, name, tm=TM, tn=TN):
    m, k = x.shape
    n = w.shape[1]
    tm, tn = _tile(m, tm), _tile(n, tn)
    return pl.pallas_call(
        _mm_kernel,
        out_shape=jax.ShapeDtypeStruct((m, n), out_dtype),
        grid=(m // tm, n // tn),
        in_specs=[pl.BlockSpec((tm, k), lambda i, j: (i, 0)), pl.BlockSpec((k, tn), lambda i, j: (0, j))],
        out_specs=pl.BlockSpec((tm, tn), lambda i, j: (i, j)),
        compiler_params=_params("parallel", "parallel"),
        name=name,
    )(x, w)


def _mm_res_kernel(x_ref, w_ref, res_ref, g_ref, o_ref):
    o_ref[...] = res_ref[...] + g_ref[...] * jnp.dot(x_ref[...], w_ref[...], preferred_element_type=F32)


def _mm_res(x, w, res, gate, seq, name, tm=TM, tn=TN_RES):
    m, k = x.shape
    n = w.shape[1]
    b = gate.shape[0]
    tm, tn = _tile(seq, tm), _tile(n, tn)
    return pl.pallas_call(
        _mm_res_kernel,
        out_shape=jax.ShapeDtypeStruct((m, n), F32),
        grid=(m // tm, n // tn),
        in_specs=[
            pl.BlockSpec((tm, k), lambda i, j: (i, 0)),
            pl.BlockSpec((k, tn), lambda i, j: (0, j)),
            pl.BlockSpec((tm, tn), lambda i, j: (i, j)),
            pl.BlockSpec((None, 1, tn), lambda i, j: (i * tm // seq, 0, j)),
        ],
        out_specs=pl.BlockSpec((tm, tn), lambda i, j: (i, j)),
        compiler_params=_params("parallel", "parallel"),
        name=name,
    )(x, w, res, gate.reshape(b, 1, n))


def _silu_mul(a, b):
    return (a * jax.nn.sigmoid(a)) * b


def _ws_glu_kernel(te_ref, tv_ref, tf_ref, x_ref, wg_ref, wu_ref, o_ref, wg_bf, wu_bf):
    i = pl.program_id(1)

    @pl.when(tf_ref[i] == 1)
    def _():
        wg_bf[...] = wg_ref[...].astype(BF16)
        wu_bf[...] = wu_ref[...].astype(BF16)

    @pl.when(tv_ref[i] == 1)
    def _():
        x = x_ref[...]
        a = jnp.dot(x, wg_bf[...], preferred_element_type=F32)
        b = jnp.dot(x, wu_bf[...], preferred_element_type=F32)
        o_ref[...] = _silu_mul(a, b).astype(o_ref.dtype)

    @pl.when(tv_ref[i] == 0)
    def _():
        o_ref[...] = jnp.zeros_like(o_ref)


def _ws_mm_kernel(te_ref, tv_ref, tf_ref, x_ref, w_ref, o_ref, w_bf):
    i = pl.program_id(1)

    @pl.when(tf_ref[i] == 1)
    def _():
        w_bf[...] = w_ref[...].astype(BF16)

    @pl.when(tv_ref[i] == 1)
    def _():
        o_ref[...] = jnp.dot(x_ref[...], w_bf[...], preferred_element_type=F32).astype(o_ref.dtype)

    @pl.when(tv_ref[i] == 0)
    def _():
        o_ref[...] = jnp.zeros_like(o_ref)


def _ws_mm(kernel, xs, ws, sched, tm, tn, out_dtype, name, n=None):
    r, k = xs.shape
    n = ws[0].shape[2] if n is None else n
    tn = _tile(n, tn)
    w_spec = pl.BlockSpec((None, k, tn), lambda j, i, te, tv, tf: (te[i], 0, j))
    return pl.pallas_call(
        kernel,
        out_shape=jax.ShapeDtypeStruct((r, n), out_dtype),
        grid_spec=pltpu.PrefetchScalarGridSpec(
            num_scalar_prefetch=3,
            grid=(n // tn, r // tm),
            in_specs=[pl.BlockSpec((tm, k), lambda j, i, te, tv, tf: (i, 0))] + [w_spec] * len(ws),
            out_specs=pl.BlockSpec((tm, tn), lambda j, i, te, tv, tf: (i, j)),
            scratch_shapes=[pltpu.VMEM((k, tn), BF16)] * len(ws),
        ),
        compiler_params=_params("parallel", "arbitrary"),
        name=name,
    )(*sched, xs, *ws)


def _dense_sched(n_tiles):
    first = (jnp.arange(n_tiles, dtype=I32) == 0).astype(I32)
    return jnp.zeros((n_tiles,), I32), jnp.ones((n_tiles,), I32), first


def _conv_gate_kernel(b_ref, c_ref, u_ref, w_ref, o_ref):
    v = c_ref[...].astype(F32) * u_ref[...].astype(F32)
    rows = lax.broadcasted_iota(I32, v.shape, 0)
    v1 = jnp.where(rows >= 1, pltpu.roll(v, 1, 0), 0.0)
    v2 = jnp.where(rows >= 2, pltpu.roll(v, 2, 0), 0.0)
    w = w_ref[...]
    y = w[0:1, :] * v2 + w[1:2, :] * v1 + w[2:3, :] * v
    o_ref[...] = (b_ref[...].astype(F32) * y).astype(o_ref.dtype)


def _conv_gate(bcu, conv_w, batch, seq):
    d = conv_w.shape[1]
    tc = _tile(d, TC_CONV)
    nc = d // tc
    bcu3 = bcu.reshape(batch, seq, 3 * d)
    out = pl.pallas_call(
        _conv_gate_kernel,
        out_shape=jax.ShapeDtypeStruct((batch, seq, d), BF16),
        grid=(batch, nc),
        in_specs=[
            pl.BlockSpec((None, seq, tc), lambda b, j: (b, 0, j)),
            pl.BlockSpec((None, seq, tc), lambda b, j: (b, 0, nc + j)),
            pl.BlockSpec((None, seq, tc), lambda b, j: (b, 0, 2 * nc + j)),
            pl.BlockSpec((CONV_WIDTH, tc), lambda b, j: (0, j)),
        ],
        out_specs=pl.BlockSpec((None, seq, tc), lambda b, j: (b, 0, j)),
        compiler_params=_params("parallel", "parallel"),
        name="conv_gate",
    )(bcu3, bcu3, bcu3, conv_w)
    return out.reshape(batch * seq, d)


def _rope_tables(pos):
    half = HEAD_DIM // 2
    inv = ROPE_THETA ** (-jnp.arange(half, dtype=F32) / half)
    ang = pos.astype(F32)[:, None] * inv[None, :]
    cos, sin = jnp.cos(ang), jnp.sin(ang)
    return jnp.concatenate([cos, cos], axis=-1), jnp.concatenate([-sin, sin], axis=-1)


def _norm_rope(x, g, cos, sin_signed):
    ms = jnp.mean(x * x, axis=-1, keepdims=True)
    xn = x * lax.rsqrt(ms + EPS) * g
    return xn * cos + pltpu.roll(xn, HEAD_DIM // 2, 1) * sin_signed


def _nsa_prep_kernel(p_ref, gp_ref, cos_ref, sin_ref, qg_ref, kg_ref,
                     q_o, ks_o, vs_o, kw_o, vw_o, g_o, *, scale):
    cos, sin = cos_ref[...], sin_ref[...]
    hd = HEAD_DIM
    qg = qg_ref[...]
    for h in range(N_HEADS):
        x = p_ref[:, h * hd:(h + 1) * hd]
        q_o[:, h * hd:(h + 1) * hd] = (_norm_rope(x, qg, cos, sin) * scale).astype(q_o.dtype)
    kv0 = N_HEADS * hd
    for g in range(N_KV_GROUPS):
        def col(i, g=g):
            c0 = kv0 + i * KV_WIDTH + g * hd
            return p_ref[:, c0:c0 + hd]
        sl = slice(g * hd, (g + 1) * hd)
        ks_o[:, sl] = _norm_rope(col(2), kg_ref[1:2, :], cos, sin).astype(ks_o.dtype)
        vs_o[:, sl] = col(3).astype(vs_o.dtype)
        kw_o[:, sl] = _norm_rope(col(4), kg_ref[2:3, :], cos, sin).astype(kw_o.dtype)
        vw_o[:, sl] = col(5).astype(vw_o.dtype)
    sig = jax.nn.sigmoid(gp_ref[...])
    per_group = 3 * HEADS_PER_GROUP
    for g in range(N_KV_GROUPS):
        shift = (LANES - g * per_group) % LANES
        g_o[:, g * LANES:(g + 1) * LANES] = sig if shift == 0 else pltpu.roll(sig, shift, 1)


def _nsa_prep(proj, gates_pre, cos, sin, q_g, k_g, seq):
    t = proj.shape[0]
    tm = _tile(seq, 256)
    nseq = seq // tm
    qd = N_HEADS * HEAD_DIM
    kvspec = pl.BlockSpec((tm, KV_WIDTH), lambda i: (i, 0))
    kvshape = jax.ShapeDtypeStruct((t, KV_WIDTH), BF16)
    return pl.pallas_call(
        functools.partial(_nsa_prep_kernel, scale=HEAD_DIM ** -0.5 * LOG2E),
        out_shape=(jax.ShapeDtypeStruct((t, qd), BF16), kvshape, kvshape, kvshape, kvshape,
                   jax.ShapeDtypeStruct((t, N_KV_GROUPS * LANES), F32)),
        grid=(t // tm,),
        in_specs=[
            pl.BlockSpec((tm, proj.shape[1]), lambda i: (i, 0)),
            pl.BlockSpec((tm, LANES), lambda i: (i, 0)),
            pl.BlockSpec((tm, HEAD_DIM), lambda i: (i % nseq, 0)),
            pl.BlockSpec((tm, HEAD_DIM), lambda i: (i % nseq, 0)),
            pl.BlockSpec((1, HEAD_DIM), lambda i: (0, 0)),
            pl.BlockSpec((3, HEAD_DIM), lambda i: (0, 0)),
        ],
        out_specs=(pl.BlockSpec((tm, qd), lambda i: (i, 0)), kvspec, kvspec, kvspec, kvspec,
                   pl.BlockSpec((tm, N_KV_GROUPS * LANES), lambda i: (i, 0))),
        compiler_params=_params("parallel"),
        name="nsa_prep",
    )(proj, gates_pre, cos, sin, q_g.reshape(1, HEAD_DIM), k_g)


def _compress_kernel(kc_ref, vc_ref, w1_ref, w2_ref, pos_ref, kg_ref, cos_ref, sin_ref,
                     ko_ref, vo_ref, *, nchunk):
    def comp(x_ref, i):
        first = jnp.zeros((nchunk, w1_ref.shape[-1]), F32)
        second = jnp.zeros_like(first)
        for l in range(CMP_STRIDE):
            xl = x_ref[pl.ds(l, nchunk, stride=CMP_STRIDE), :]
            xa = (xl + pos_ref[i, l:l + 1, :]).astype(BF16)
            xb = (xl + pos_ref[i, CMP_STRIDE + l:CMP_STRIDE + l + 1, :]).astype(BF16)
            first += jnp.dot(xa, w1_ref[i, l], preferred_element_type=F32)
            second += jnp.dot(xb, w1_ref[i, CMP_STRIDE + l], preferred_element_type=F32)
        pre = first + pltpu.roll(second, nchunk - 1, 0)
        hid = pre * jax.nn.sigmoid(pre)
        return jnp.dot(hid.astype(BF16), w2_ref[i], preferred_element_type=F32)

    k = comp(kc_ref, 0)
    ko_ref[...] = _norm_rope(k, kg_ref[...], cos_ref[...], sin_ref[...]).astype(ko_ref.dtype)
    vo_ref[...] = comp(vc_ref, 1).astype(vo_ref.dtype)


def _compress(proj3, w1, w2, cmp_pos, kg0, cos_c, sin_c):
    batch, seq, _ = proj3.shape
    nchunk = seq // CMP_STRIDE
    hd = HEAD_DIM
    cb = N_HEADS
    out = jax.ShapeDtypeStruct((batch, N_KV_GROUPS, nchunk, hd), BF16)
    ospec = pl.BlockSpec((None, None, nchunk, hd), lambda b, g: (b, g, 0, 0))
    full = lambda a: pl.BlockSpec(a.shape, lambda b, g: (0,) * a.ndim)
    return pl.pallas_call(
        functools.partial(_compress_kernel, nchunk=nchunk),
        out_shape=(out, out),
        grid=(batch, N_KV_GROUPS),
        in_specs=[
            pl.BlockSpec((None, seq, hd), lambda b, g: (b, 0, cb + g)),
            pl.BlockSpec((None, seq, hd), lambda b, g: (b, 0, cb + N_KV_GROUPS + g)),
            full(w1), full(w2), full(cmp_pos), full(kg0), full(cos_c), full(sin_c),
        ],
        out_specs=(ospec, ospec),
        compiler_params=_params("parallel", "parallel"),
        name="nsa_compress",
    )(proj3, proj3, w1, w2, cmp_pos, kg0, cos_c, sin_c)


def _nsa_attn_kernel(q_ref, kc_ref, vc_ref, ks_ref, vs_ref, kw_ref, vw_ref, g_ref, ov_ref, e_ref,
                     o_ref, p_ref, b_ref, m_ref, l_ref, a_ref, acc_ref,
                     *, tq, tk, seq, n_sel, n_top, span):
    J = HEADS_PER_GROUP
    hd = HEAD_DIM
    q0 = pl.program_id(2) * tq
    q = q_ref[...]
    qa = jnp.concatenate([q[:, j * hd:(j + 1) * hd] for j in range(J)], axis=0)
    tpos = q0 + lax.broadcasted_iota(I32, (tq, 1), 0)

    def scores(kblk):
        return lax.dot_general(qa, kblk, (((1,), (1,)), ((), ())), preferred_element_type=F32)

    ncp = kc_ref.shape[0]
    n_idx = lax.broadcasted_iota(I32, (1, ncp), 1)
    mask_c = (n_idx * CMP_STRIDE + (CMP_BLOCK - 1)) <= tpos
    s3 = jnp.where(mask_c[None], scores(kc_ref[...]).reshape(J, tq, ncp), NEG_INF)
    m = jnp.max(s3, axis=-1, keepdims=True)
    p = jnp.where(mask_c[None], jnp.exp2(s3 - m), 0.0)
    l = jnp.sum(p, axis=-1, keepdims=True)
    pc = p * jnp.where(l > 0.0, 1.0 / l, 0.0)
    o_c = jnp.dot(pc.reshape(J * tq, ncp).astype(BF16), vc_ref[...], preferred_element_type=F32)

    nsp = e_ref.shape[1]
    imp = lax.dot_general(ov_ref[...], jnp.sum(pc, axis=0), (((1,), (1,)), ((), ())),
                          preferred_element_type=F32, precision=lax.Precision.HIGHEST)[:nsp]
    blk = lax.broadcasted_iota(I32, (nsp, 1), 0)
    tpos_l = q0 + lax.broadcasted_iota(I32, (1, tq), 1)
    cur = tpos_l // SEL_BLOCK
    forced = (blk == 0) | (blk == cur) | (blk == cur - 1)
    imp = jnp.where(blk * SEL_BLOCK <= tpos_l, imp + FORCE_BONUS * forced.astype(F32), NEG_INF)
    rank = jnp.zeros((nsp, tq), F32)
    for mp in range(n_sel):
        row = imp[mp:mp + 1, :]
        rank += jnp.where(blk > mp, (row >= imp).astype(F32), (row > imp).astype(F32))
    sel = jnp.transpose(((rank < n_top) & (blk < n_sel)).astype(F32)).astype(BF16)

    rb = ATTN_ROW_BLOCK

    def softmax_head(j, kblk, width, online):
        s_head = lax.dot_general(q[:, j * hd:(j + 1) * hd], kblk, (((1,), (1,)), ((), ())),
                                 preferred_element_type=F32)
        lane_tiles = [slice(c, c + LANES) for c in range(0, width, LANES)]
        for h0 in range(0, tq, rb):
            rows = slice(j * tq + h0, j * tq + h0 + rb)
            s = [s_head[h0:h0 + rb, c] + b_ref[h0:h0 + rb, c] for c in lane_tiles]
            m_new = jnp.broadcast_to(jnp.max(functools.reduce(jnp.maximum, s), axis=-1, keepdims=True),
                                     (rb, LANES))
            if online:
                m_old = m_ref[rows, :]
                m_new = jnp.maximum(m_old, m_new)
                alpha = jnp.exp2(m_old - m_new)
                m_ref[rows, :] = m_new
                a_ref[rows, :] = alpha
            p = [jnp.exp2(x - m_new) for x in s]
            row_sum = jnp.broadcast_to(jnp.sum(functools.reduce(jnp.add, p), axis=-1, keepdims=True),
                                       (rb, LANES))
            l_ref[rows, :] = alpha * l_ref[rows, :] + row_sum if online else row_sum
            for c, x in zip(lane_tiles, p):
                p_ref[rows, c] = x.astype(BF16)

    m_ref[...] = jnp.full(m_ref.shape, NEG_INF, F32)
    l_ref[...] = jnp.zeros(l_ref.shape, F32)
    acc_ref[...] = jnp.zeros(acc_ref.shape, F32)

    def sel_chunk(c, carry):
        k0 = pl.multiple_of(c * tk, tk)
        picked = jnp.dot(sel, e_ref[c], preferred_element_type=F32)
        kpos = k0 + lax.broadcasted_iota(I32, (1, tk), 1)
        b_ref[:, :tk] = jnp.where((picked > 0.5) & (kpos <= tpos), 0.0, NEG_INF)
        kblk = ks_ref[pl.ds(k0, tk), :]
        for j in range(J):
            softmax_head(j, kblk, tk, online=True)
        pv = jnp.dot(p_ref[:, :tk], vs_ref[pl.ds(k0, tk), :], preferred_element_type=F32)
        acc_ref[...] = a_ref[...] * acc_ref[...] + pv
        return carry

    lax.fori_loop(0, (q0 + tq + tk - 1) // tk, sel_chunk, 0)
    inv_ls = 1.0 / l_ref[...]

    start = pl.multiple_of(jnp.maximum(q0 + tq - span, 0), tq)
    diff = tpos - (start + lax.broadcasted_iota(I32, (1, span), 1))
    b_ref[:, :span] = jnp.where((diff >= 0) & (diff < WINDOW), 0.0, NEG_INF)
    kblk = kw_ref[pl.ds(start, span), :]
    for j in range(J):
        softmax_head(j, kblk, span, online=False)
    o_w = jnp.dot(p_ref[:, :span], vw_ref[pl.ds(start, span), :], preferred_element_type=F32)
    inv_lw = 1.0 / l_ref[...]

    gt = g_ref[...]
    for j in range(J):
        rows = slice(j * tq, (j + 1) * tq)
        o = (gt[:, 3 * j:3 * j + 1] * o_c[rows]
             + (gt[:, 3 * j + 1:3 * j + 2] * inv_ls[rows]) * acc_ref[rows, :]
             + (gt[:, 3 * j + 2:3 * j + 3] * inv_lw[rows]) * o_w[rows])
        o_ref[:, j * hd:(j + 1) * hd] = o.astype(o_ref.dtype)


def _nsa_attn(qr, kcmp, vcmp, ks, vs, kw, vw, gates, batch, seq):
    hd = HEAD_DIM
    J = HEADS_PER_GROUP
    tq = _tile(seq, TQ)
    tk = _tile(seq, TK_SEL)
    n_sel = seq // SEL_BLOCK
    n_top = min(SEL_TOP_N, n_sel)
    span = min(WINDOW + tq, seq)
    width = max(span, tk)
    ncp = kcmp.shape[2]
    nsp = -(-n_sel // 16) * 16
    ci = jnp.arange(ncp, dtype=I32)[None, :] * CMP_STRIDE
    sj = jnp.arange(LANES, dtype=I32)[:, None] * SEL_BLOCK
    ov = jnp.clip(jnp.minimum(ci + CMP_BLOCK, sj + SEL_BLOCK) - jnp.maximum(ci, sj), 0).astype(F32) / CMP_BLOCK
    ov = jnp.where((jnp.arange(ncp)[None, :] < ncp - 1) & (jnp.arange(LANES)[:, None] < n_sel), ov, 0.0)
    kk = jnp.arange(seq, dtype=I32).reshape(seq // tk, 1, tk)
    expand = (kk // SEL_BLOCK == jnp.arange(nsp, dtype=I32)[None, :, None]).astype(BF16)

    r3 = lambda a: a.reshape(batch, seq, a.shape[-1])
    kv_spec = pl.BlockSpec((None, seq, hd), lambda b, g, i: (b, 0, g))
    cmp_spec = pl.BlockSpec((None, None, ncp, hd), lambda b, g, i: (b, g, 0, 0))
    out = pl.pallas_call(
        functools.partial(_nsa_attn_kernel, tq=tq, tk=tk, seq=seq, n_sel=n_sel, n_top=n_top, span=span),
        out_shape=jax.ShapeDtypeStruct((batch, seq, N_HEADS * hd), BF16),
        grid=(batch, N_KV_GROUPS, seq // tq),
        in_specs=[
            pl.BlockSpec((None, tq, J * hd), lambda b, g, i: (b, i, g)),
            cmp_spec, cmp_spec, kv_spec, kv_spec, kv_spec, kv_spec,
            pl.BlockSpec((None, tq, LANES), lambda b, g, i: (b, i, g)),
            pl.BlockSpec(ov.shape, lambda b, g, i: (0, 0)),
            pl.BlockSpec(expand.shape, lambda b, g, i: (0, 0, 0)),
        ],
        out_specs=pl.BlockSpec((None, tq, J * hd), lambda b, g, i: (b, i, g)),
        scratch_shapes=[
            pltpu.VMEM((J * tq, width), BF16),
            pltpu.VMEM((tq, width), F32),
            pltpu.VMEM((J * tq, LANES), F32),
            pltpu.VMEM((J * tq, LANES), F32),
            pltpu.VMEM((J * tq, LANES), F32),
            pltpu.VMEM((J * tq, hd), F32),
        ],
        compiler_params=_params("parallel", "parallel", "parallel"),
        name="nsa_attn",
    )(r3(qr), kcmp, vcmp, r3(ks), r3(vs), r3(kw), r3(vw), r3(gates), ov, expand)
    return out.reshape(batch * seq, N_HEADS * hd)


def _route_kernel(lg_ref, tri_ref, mi_ref, mw_ref, cnt_ref, carry_ref):
    @pl.when(pl.program_id(0) == 0)
    def _():
        carry_ref[...] = jnp.zeros_like(carry_ref)

    lane = lax.broadcasted_iota(I32, lg_ref.shape, 1)
    lg = jnp.where(lane < N_EXPERTS, lg_ref[...], -jnp.inf)
    m1 = jnp.max(lg, axis=-1, keepdims=True)
    i1 = jnp.min(jnp.where(lg == m1, lane, LANES), axis=-1, keepdims=True)
    lg2 = jnp.where(lane == i1, -jnp.inf, lg)
    m2 = jnp.max(lg2, axis=-1, keepdims=True)
    i2 = jnp.min(jnp.where(lg2 == m2, lane, LANES), axis=-1, keepdims=True)
    e2 = jnp.exp(m2 - m1)
    w1 = 1.0 / (1.0 + e2)
    w2 = e2 / (1.0 + e2)
    oh1, oh2 = lane == i1, lane == i2
    onehot = (oh1 | oh2).astype(F32)
    before = jnp.dot(tri_ref[...], onehot.astype(BF16), preferred_element_type=F32) + carry_ref[...]
    pos1 = jnp.sum(jnp.where(oh1, before, 0.0), axis=-1, keepdims=True).astype(I32)
    pos2 = jnp.sum(jnp.where(oh2, before, 0.0), axis=-1, keepdims=True).astype(I32)
    carry_ref[...] += jnp.sum(onehot, axis=0, keepdims=True)
    mi_ref[...] = jnp.where(lane == 0, i1, jnp.where(lane == 1, i2,
                            jnp.where(lane == 2, pos1, jnp.where(lane == 3, pos2, 0))))
    mw_ref[...] = jnp.where(lane == 0, w1, jnp.where(lane == 1, w2, 0.0))
    cnt_ref[...] = carry_ref[...]


def _route(logits):
    t = logits.shape[0]
    tm = _tile(t, 512)
    tri = (jnp.arange(tm)[None, :] < jnp.arange(tm)[:, None]).astype(BF16)
    return pl.pallas_call(
        _route_kernel,
        out_shape=(jax.ShapeDtypeStruct((t, LANES), I32), jax.ShapeDtypeStruct((t, LANES), F32),
                   jax.ShapeDtypeStruct((1, LANES), F32)),
        grid=(t // tm,),
        in_specs=[pl.BlockSpec((tm, LANES), lambda i: (i, 0)), pl.BlockSpec((tm, tm), lambda i: (0, 0))],
        out_specs=(pl.BlockSpec((tm, LANES), lambda i: (i, 0)), pl.BlockSpec((tm, LANES), lambda i: (i, 0)),
                   pl.BlockSpec((1, LANES), lambda i: (0, 0))),
        scratch_shapes=[pltpu.VMEM((1, LANES), F32)],
        compiler_params=_params("arbitrary"),
        name="moe_route",
    )(logits, tri)


def _row_copy(src_hbm, row, buf, slot, r, sem):
    return pltpu.make_async_copy(src_hbm.at[pl.ds(row, 1), :], buf.at[slot, pl.ds(r, 1), :], sem.at[slot])


def _issue_rows(idx_ref, src_hbm, buf, slot, base, sem, rows):
    def body(r, carry):
        _row_copy(src_hbm, idx_ref[0, r], buf, slot, base + r, sem).start()
        return carry

    lax.fori_loop(0, rows, body, 0, unroll=8)


def _wait_rows(src_hbm, buf, slot, sem, rows):
    def body(r, carry):
        _row_copy(src_hbm, 0, buf, slot, r, sem).wait()
        return carry

    lax.fori_loop(0, rows, body, 0, unroll=8)


def _dispatch_kernel(idx_ref, nxt_ref, src_hbm, o_ref, buf, sem, *, rows):
    i = pl.program_id(0)
    slot = i % 2

    @pl.when(i == 0)
    def _():
        _issue_rows(idx_ref, src_hbm, buf, 0, 0, sem, rows)

    @pl.when(i + 1 < pl.num_programs(0))
    def _():
        _issue_rows(nxt_ref, src_hbm, buf, 1 - slot, 0, sem, rows)

    _wait_rows(src_hbm, buf, slot, sem, rows)
    o_ref[...] = buf[slot].astype(o_ref.dtype)


def _dispatch_rows(src, idx):
    n = idx.shape[0]
    d = src.shape[1]
    tg = _tile(n, TG)
    steps = n // tg
    smem = lambda f: pl.BlockSpec((None, 1, tg), f, memory_space=pltpu.SMEM)
    idx3 = idx.reshape(steps, 1, tg)
    return pl.pallas_call(
        functools.partial(_dispatch_kernel, rows=tg),
        out_shape=jax.ShapeDtypeStruct((n, d), BF16),
        grid=(steps,),
        in_specs=[
            smem(lambda i: (i, 0, 0)),
            smem(lambda i: (jnp.minimum(i + 1, steps - 1), 0, 0)),
            pl.BlockSpec(memory_space=pl.ANY),
        ],
        out_specs=pl.BlockSpec((tg, d), lambda i: (i, 0)),
        scratch_shapes=[pltpu.VMEM((2, tg, d), F32), pltpu.SemaphoreType.DMA((2,))],
        compiler_params=_params("arbitrary"),
        name="moe_dispatch",
    )(idx3, idx3, src)


def _combine_kernel(a_ref, b_ref, na_ref, nb_ref, y_hbm, x_ref, g_ref, mw_ref, o_ref, buf, sem, *, rows):
    i = pl.program_id(0)
    slot = i % 2

    def issue(first_ref, second_ref, s):
        _issue_rows(first_ref, y_hbm, buf, s, 0, sem, rows)
        _issue_rows(second_ref, y_hbm, buf, s, rows, sem, rows)

    @pl.when(i == 0)
    def _():
        issue(a_ref, b_ref, 0)

    @pl.when(i + 1 < pl.num_programs(0))
    def _():
        issue(na_ref, nb_ref, 1 - slot)

    _wait_rows(y_hbm, buf, slot, sem, 2 * rows)
    mw = mw_ref[...]
    f = mw[:, 0:1] * buf[slot, 0:rows, :] + mw[:, 1:2] * buf[slot, rows:2 * rows, :]
    o_ref[...] = x_ref[...] + g_ref[...] * f


def _combine_rows(ys, d1, d2, meta_w, x, gate, seq):
    t, d = x.shape
    b = gate.shape[0]
    tc = _tile(seq, TCMB)
    steps = t // tc
    smem = lambda f: pl.BlockSpec((None, 1, tc), f, memory_space=pltpu.SMEM)
    cur, nxt = (lambda i: (i, 0, 0)), (lambda i: (jnp.minimum(i + 1, steps - 1), 0, 0))
    a3, b3 = d1.reshape(steps, 1, tc), d2.reshape(steps, 1, tc)
    return pl.pallas_call(
        functools.partial(_combine_kernel, rows=tc),
        out_shape=jax.ShapeDtypeStruct((t, d), F32),
        grid=(steps,),
        in_specs=[
            smem(cur), smem(cur), smem(nxt), smem(nxt),
            pl.BlockSpec(memory_space=pl.ANY),
            pl.BlockSpec((tc, d), lambda i: (i, 0)),
            pl.BlockSpec((None, 1, d), lambda i: (i * tc // seq, 0, 0)),
            pl.BlockSpec((tc, LANES), lambda i: (i, 0)),
        ],
        out_specs=pl.BlockSpec((tc, d), lambda i: (i, 0)),
        scratch_shapes=[pltpu.VMEM((2, 2 * tc, d), F32), pltpu.SemaphoreType.DMA((2,))],
        compiler_params=_params("arbitrary"),
        name="moe_combine",
    )(a3, b3, a3, b3, ys, x, gate.reshape(b, 1, d), meta_w)


def _moe_ffn_residual(x, hf, logits, gate, w_gate, w_up, w_down, seq):
    t, d = hf.shape
    n_exp = w_gate.shape[0]
    tm = _tile(t, TM_MOE)
    meta_i, meta_w, counts = _route(logits)
    e1, e2, p1, p2 = meta_i[:, 0], meta_i[:, 1], meta_i[:, 2], meta_i[:, 3]
    cnt = counts[0, :n_exp].astype(I32)
    padded = (cnt + tm - 1) // tm * tm
    ends = jnp.cumsum(padded)
    starts = ends - padded
    d1, d2 = starts[e1] + p1, starts[e2] + p2
    n_rows = 2 * t + n_exp * tm
    tok = jnp.arange(t, dtype=I32)
    src = jnp.zeros((n_rows,), I32).at[d1].set(tok).at[d2].set(tok)
    tile_start = jnp.arange(n_rows // tm, dtype=I32) * tm
    tile_expert = jnp.minimum(jnp.sum(tile_start[:, None] >= ends[None, :], axis=1), n_exp - 1).astype(I32)
    tile_valid = (tile_start < ends[-1]).astype(I32)
    tile_first = jnp.concatenate([jnp.ones((1,), I32), (tile_expert[1:] != tile_expert[:-1]).astype(I32)])
    sched = (tile_expert, tile_valid, tile_first)

    xs = _dispatch_rows(hf, src)
    hs = _ws_mm(_ws_glu_kernel, xs, (w_gate, w_up), sched, tm, TN_MOE, BF16, "moe_glu")
    ys = _ws_mm(_ws_mm_kernel, hs, (w_down,), sched, tm, TN_MOE_DOWN, F32, "moe_down")
    return _combine_rows(ys, d1, d2, meta_w, x, gate, seq)


def kernel(x, c, ada_w, ada_b, norm_mix_g, norm_ffn_g, conv_w_in, conv_w, conv_w_out, nsa_w_in, nsa_w_o,
           nsa_q_norm_g, nsa_k_norm_g, cmp_pos, cmp_w1, cmp_w2, ffn_w_gate, ffn_w_up, ffn_w_down,
           router_w, router_b, moe_w_gate, moe_w_up, moe_w_down):
    batch, seq, d = x.shape
    t = batch * seq
    bf = lambda a: a.astype(BF16)

    mod = _ada_mod(c, ada_w, ada_b)
    sh_m, sc_m, g_m, sh_f, sc_f, g_f = [mod[:, :, i * d:(i + 1) * d] for i in range(6)]
    xt = x.reshape(t, d)

    hm = _norm_mod(xt, norm_mix_g[0], sc_m[0], sh_m[0], seq)
    bcu = _mm(hm, bf(conv_w_in[0]), BF16, "conv_in")
    z = _conv_gate(bcu, conv_w[0], batch, seq)
    xt = _mm_res(z, bf(conv_w_out[0]), xt, g_m[0], seq, "conv_out")
    hf = _norm_mod(xt, norm_ffn_g[0], sc_f[0], sh_f[0], seq)
    tm_ffn = _tile(t, 2 * TM)
    hid = _ws_mm(_ws_glu_kernel, hf, (ffn_w_gate[0:1], ffn_w_up[0:1]), _dense_sched(t // tm_ffn), tm_ffn,
                 TN_GLU, BF16, "ffn_glu")
    xt = _mm_res(hid, bf(ffn_w_down[0]), xt, g_f[0], seq, "ffn_down", tm=TM_DOWN, tn=TN_DOWN)

    hm = _norm_mod(xt, norm_mix_g[1], sc_m[1], sh_m[1], seq)
    qkv_w = N_HEADS * HEAD_DIM + 6 * KV_WIDTH
    tm_in = _tile(t, TM)
    proj = _ws_mm(_ws_mm_kernel, hm, (nsa_w_in[0:1],), _dense_sched(t // tm_in), tm_in, TN_MOE, F32,
                  "nsa_in", n=qkv_w)
    w_gates = jnp.pad(nsa_w_in[0, :, qkv_w:], ((0, 0), (0, LANES - 3 * N_HEADS)))
    gates_pre = _mm(hm, bf(w_gates), F32, "nsa_gates")
    pos = jnp.arange(seq, dtype=I32)
    cos, sin = _rope_tables(pos)
    qr, ks, vs, kw, vw, gates = _nsa_prep(proj, gates_pre, cos, sin, nsa_q_norm_g[0], nsa_k_norm_g[0], seq)
    cmp_end = jnp.arange(seq // CMP_STRIDE, dtype=I32) * CMP_STRIDE + CMP_BLOCK - 1
    cos_c, sin_c = _rope_tables(cmp_end)
    kcmp, vcmp = _compress(proj.reshape(batch, seq, qkv_w), bf(cmp_w1[0]), bf(cmp_w2[0]), cmp_pos[0],
                           nsa_k_norm_g[0, 0:1], cos_c, sin_c)
    attn = _nsa_attn(qr, kcmp, vcmp, ks, vs, kw, vw, gates, batch, seq)
    xt = _mm_res(attn, bf(nsa_w_o[0]), xt, g_m[1], seq, "nsa_out")

    n_exp = router_w.shape[2]
    rw = jnp.pad(router_w[0], ((0, 0), (0, LANES - n_exp)))
    rb = jnp.pad(router_b[0], (0, LANES - n_exp)).reshape(1, LANES)
    hf, logits = _norm_mod(xt, norm_ffn_g[1], sc_f[1], sh_f[1], seq, router=(rw, rb))
    out = _moe_ffn_residual(xt, hf, logits, g_f[1], moe_w_gate[0], moe_w_up[0], moe_w_down[0], seq)
    return out.reshape(batch, seq, d)
```

```python
import functools

import jax
import jax.numpy as jnp
from jax import lax
from jax.experimental import pallas as pl
from jax.experimental.pallas import tpu as pltpu

F32 = jnp.float32
BF16 = jnp.bfloat16
I32 = jnp.int32

EPS = 1e-6
NEG_INF = -1e30
FORCE_BONUS = 1e4
ROPE_THETA = 10000.0
LOG2E = 1.4426950408889634

CONV_WIDTH = 3
N_HEADS = 32
HEAD_DIM = 128
N_KV_GROUPS = 4
HEADS_PER_GROUP = N_HEADS // N_KV_GROUPS
KV_WIDTH = N_KV_GROUPS * HEAD_DIM
CMP_BLOCK = 32
CMP_STRIDE = 16
SEL_BLOCK = 64
SEL_TOP_N = 16
WINDOW = 512
N_EXPERTS = 8

LANES = 128
VMEM_LIMIT_V7X = 60 * 1024 * 1024

TM = 1024
TN = 1024
TN_RES = 1024
TN_GLU = 256
TM_DOWN = 512
TN_DOWN = 512
TM_ROW = 512
TC_CONV = 512
TQ = 256
TK_SEL = 512
ATTN_ROW_BLOCK = 64
TM_MOE = 512
TN_MOE = 512
TN_MOE_DOWN = 1024
TG = 256
TCMB = 128


def _params(*sem):
    return pltpu.CompilerParams(dimension_semantics=sem, vmem_limit_bytes=VMEM_LIMIT_V7X)


def _tile(full, want):
    return want if full % want == 0 else full


def _ada_kernel(c_ref, w_ref, b_ref, o_ref):
    c = c_ref[...]
    cond = c * jax.nn.sigmoid(c)
    acc = jnp.dot(cond.astype(BF16), w_ref[...].astype(BF16), preferred_element_type=F32)
    o_ref[...] = acc + b_ref[...]


def _ada_mod(c, ada_w, ada_b):
    depth, d, n = ada_w.shape
    b = c.shape[0]
    tn = _tile(n, 512)
    return pl.pallas_call(
        _ada_kernel,
        out_shape=jax.ShapeDtypeStruct((depth, b, n), F32),
        grid=(depth, n // tn),
        in_specs=[
            pl.BlockSpec((b, d), lambda l, j: (0, 0)),
            pl.BlockSpec((None, d, tn), lambda l, j: (l, 0, j)),
            pl.BlockSpec((None, 1, tn), lambda l, j: (l, 0, j)),
        ],
        out_specs=pl.BlockSpec((None, b, tn), lambda l, j: (l, 0, j)),
        compiler_params=_params("parallel", "parallel"),
        name="ada_mod",
    )(c, ada_w, ada_b.reshape(depth, 1, n))


def _norm_mod_value(x_ref, g_ref, sc_ref, sh_ref):
    x = x_ref[...]
    ms = jnp.mean(x * x, axis=-1, keepdims=True)
    y = x * lax.rsqrt(ms + EPS) * g_ref[...]
    return y * (1.0 + sc_ref[...]) + sh_ref[...]


def _norm_mod_kernel(x_ref, g_ref, sc_ref, sh_ref, o_ref):
    o_ref[...] = _norm_mod_value(x_ref, g_ref, sc_ref, sh_ref).astype(o_ref.dtype)


def _norm_mod_router_kernel(x_ref, g_ref, sc_ref, sh_ref, rw_ref, rb_ref, o_ref, lg_ref):
    h = _norm_mod_value(x_ref, g_ref, sc_ref, sh_ref)
    o_ref[...] = h.astype(o_ref.dtype)
    lg_ref[...] = jnp.dot(h, rw_ref[...], preferred_element_type=F32,
                          precision=lax.Precision.HIGHEST) + rb_ref[...]


def _norm_mod(x, g, sc, sh, seq, router=None):
    t, d = x.shape
    b = sc.shape[0]
    tm = _tile(seq, TM_ROW)
    row_specs = [
        pl.BlockSpec((tm, d), lambda i: (i, 0)),
        pl.BlockSpec((1, d), lambda i: (0, 0)),
        pl.BlockSpec((None, 1, d), lambda i: (i * tm // seq, 0, 0)),
        pl.BlockSpec((None, 1, d), lambda i: (i * tm // seq, 0, 0)),
    ]
    args = [x, g.reshape(1, d), sc.reshape(b, 1, d), sh.reshape(b, 1, d)]
    if router is None:
        return pl.pallas_call(
            _norm_mod_kernel,
            out_shape=jax.ShapeDtypeStruct((t, d), BF16),
            grid=(t // tm,),
            in_specs=row_specs,
            out_specs=pl.BlockSpec((tm, d), lambda i: (i, 0)),
            compiler_params=_params("parallel"),
            name="norm_mod",
        )(*args)
    rw, rb = router
    return pl.pallas_call(
        _norm_mod_router_kernel,
        out_shape=(jax.ShapeDtypeStruct((t, d), F32), jax.ShapeDtypeStruct((t, LANES), F32)),
        grid=(t // tm,),
        in_specs=row_specs + [
            pl.BlockSpec((d, LANES), lambda i: (0, 0)),
            pl.BlockSpec((1, LANES), lambda i: (0, 0)),
        ],
        out_specs=(pl.BlockSpec((tm, d), lambda i: (i, 0)), pl.BlockSpec((tm, LANES), lambda i: (i, 0))),
        compiler_params=_params("parallel"),
        name="norm_mod_router",
    )(*args, rw, rb)


def _mm_kernel(x_ref, w_ref, o_ref):
    o_ref[...] = jnp.dot(x_ref[...], w_ref[...], preferred_element_type=F32).astype(o_ref.dtype)


def _mm(x, w, out_dtype, name, tm=TM, tn=TN):
    m, k = x.shape
    n = w.shape[1]
    tm, tn = _tile(m, tm), _tile(n, tn)
    return pl.pallas_call(
        _mm_kernel,
        out_shape=jax.ShapeDtypeStruct((m, n), out_dtype),
        grid=(m // tm, n // tn),
        in_specs=[pl.BlockSpec((tm, k), lambda i, j: (i, 0)), pl.BlockSpec((k, tn), lambda i, j: (0, j))],
        out_specs=pl.BlockSpec((tm, tn), lambda i, j: (i, j)),
        compiler_params=_params("parallel", "parallel"),
        name=name,
    )(x, w)


def _mm_res_kernel(x_ref, w_ref, res_ref, g_ref, o_ref):
    o_ref[...] = res_ref[...] + g_ref[...] * jnp.dot(x_ref[...], w_ref[...], preferred_element_type=F32)


def _mm_res(x, w, res, gate, seq, name, tm=TM, tn=TN_RES):
    m, k = x.shape
    n = w.shape[1]
    b = gate.shape[0]
    tm, tn = _tile(seq, tm), _tile(n, tn)
    return pl.pallas_call(
        _mm_res_kernel,
        out_shape=jax.ShapeDtypeStruct((m, n), F32),
        grid=(m // tm, n // tn),
        in_specs=[
            pl.BlockSpec((tm, k), lambda i, j: (i, 0)),
            pl.BlockSpec((k, tn), lambda i, j: (0, j)),
            pl.BlockSpec((tm, tn), lambda i, j: (i, j)),
            pl.BlockSpec((None, 1, tn), lambda i, j: (i * tm // seq, 0, j)),
        ],
        out_specs=pl.BlockSpec((tm, tn), lambda i, j: (i, j)),
        compiler_params=_params("parallel", "parallel"),
        name=name,
    )(x, w, res, gate.reshape(b, 1, n))


def _silu_mul(a, b):
    return (a * jax.nn.sigmoid(a)) * b


def _ws_glu_kernel(te_ref, tr_ref, tf_ref, x_ref, wg_ref, wu_ref, o_ref, wg_bf, wu_bf):
    i = pl.program_id(1)

    @pl.when(tf_ref[i] == 1)
    def _():
        wg_bf[...] = wg_ref[...].astype(BF16)
        wu_bf[...] = wu_ref[...].astype(BF16)

    @pl.when(tr_ref[i] == i)
    def _():
        x = x_ref[...]
        a = jnp.dot(x, wg_bf[...], preferred_element_type=F32)
        b = jnp.dot(x, wu_bf[...], preferred_element_type=F32)
        o_ref[...] = _silu_mul(a, b).astype(o_ref.dtype)

    @pl.when(tr_ref[i] != i)
    def _():
        o_ref[...] = jnp.zeros_like(o_ref)


def _ws_mm_kernel(te_ref, tr_ref, tf_ref, x_ref, w_ref, o_ref, w_bf):
    i = pl.program_id(1)

    @pl.when(tf_ref[i] == 1)
    def _():
        w_bf[...] = w_ref[...].astype(BF16)

    @pl.when(tr_ref[i] == i)
    def _():
        o_ref[...] = jnp.dot(x_ref[...], w_bf[...], preferred_element_type=F32).astype(o_ref.dtype)

    @pl.when(tr_ref[i] != i)
    def _():
        o_ref[...] = jnp.zeros_like(o_ref)


def _ws_mm(kernel, xs, ws, sched, tm, tn, out_dtype, name):
    r, k = xs.shape
    n = ws[0].shape[2]
    tn = _tile(n, tn)
    w_spec = pl.BlockSpec((None, k, tn), lambda j, i, te, tr, tf: (te[i], 0, j))
    return pl.pallas_call(
        kernel,
        out_shape=jax.ShapeDtypeStruct((r, n), out_dtype),
        grid_spec=pltpu.PrefetchScalarGridSpec(
            num_scalar_prefetch=3,
            grid=(n // tn, r // tm),
            in_specs=[pl.BlockSpec((tm, k), lambda j, i, te, tr, tf: (tr[i], 0))] + [w_spec] * len(ws),
            out_specs=pl.BlockSpec((tm, tn), lambda j, i, te, tr, tf: (i, j)),
            scratch_shapes=[pltpu.VMEM((k, tn), BF16)] * len(ws),
        ),
        compiler_params=_params("parallel", "arbitrary"),
        name=name,
    )(*sched, xs, *ws)


def _dense_sched(n_tiles):
    tiles = jnp.arange(n_tiles, dtype=I32)
    return jnp.zeros((n_tiles,), I32), tiles, (tiles == 0).astype(I32)


def _conv_gate_kernel(b_ref, c_ref, u_ref, w_ref, o_ref):
    v = c_ref[...].astype(F32) * u_ref[...].astype(F32)
    rows = lax.broadcasted_iota(I32, v.shape, 0)
    v1 = jnp.where(rows >= 1, pltpu.roll(v, 1, 0), 0.0)
    v2 = jnp.where(rows >= 2, pltpu.roll(v, 2, 0), 0.0)
    w = w_ref[...]
    y = w[0:1, :] * v2 + w[1:2, :] * v1 + w[2:3, :] * v
    o_ref[...] = (b_ref[...].astype(F32) * y).astype(o_ref.dtype)


def _conv_gate(bcu, conv_w, batch, seq):
    d = conv_w.shape[1]
    tc = _tile(d, TC_CONV)
    nc = d // tc
    bcu3 = bcu.reshape(batch, seq, 3 * d)
    out = pl.pallas_call(
        _conv_gate_kernel,
        out_shape=jax.ShapeDtypeStruct((batch, seq, d), BF16),
        grid=(batch, nc),
        in_specs=[
            pl.BlockSpec((None, seq, tc), lambda b, j: (b, 0, j)),
            pl.BlockSpec((None, seq, tc), lambda b, j: (b, 0, nc + j)),
            pl.BlockSpec((None, seq, tc), lambda b, j: (b, 0, 2 * nc + j)),
            pl.BlockSpec((CONV_WIDTH, tc), lambda b, j: (0, j)),
        ],
        out_specs=pl.BlockSpec((None, seq, tc), lambda b, j: (b, 0, j)),
        compiler_params=_params("parallel", "parallel"),
        name="conv_gate",
    )(bcu3, bcu3, bcu3, conv_w)
    return out.reshape(batch * seq, d)


def _rope_tables(pos):
    half = HEAD_DIM // 2
    inv = ROPE_THETA ** (-jnp.arange(half, dtype=F32) / half)
    ang = pos.astype(F32)[:, None] * inv[None, :]
    cos, sin = jnp.cos(ang), jnp.sin(ang)
    return jnp.concatenate([cos, cos], axis=-1), jnp.concatenate([-sin, sin], axis=-1)


def _norm_rope(x, g, cos, sin_signed, ones=None):
    if ones is None:
        ms = jnp.mean(x * x, axis=-1, keepdims=True)
    else:
        ms = jnp.dot((x * x).astype(BF16), ones, preferred_element_type=F32) * (1.0 / HEAD_DIM)
    xn = x * lax.rsqrt(ms + EPS) * g
    return xn * cos + pltpu.roll(xn, HEAD_DIM // 2, 1) * sin_signed


def _nsa_prep_kernel(p_ref, gp_ref, cos_ref, sin_ref, qg_ref, kg_ref,
                     q_o, ks_o, vs_o, kw_o, vw_o, g_o, *, scale):
    cos, sin = cos_ref[...], sin_ref[...]
    hd = HEAD_DIM
    qg = qg_ref[...]
    ones = jnp.ones((hd, hd), BF16)
    for h in range(N_HEADS):
        x = p_ref[:, h * hd:(h + 1) * hd]
        q_o[:, h * hd:(h + 1) * hd] = (_norm_rope(x, qg, cos, sin, ones) * scale).astype(q_o.dtype)
    kv0 = N_HEADS * hd
    for g in range(N_KV_GROUPS):
        def col(i, g=g):
            c0 = kv0 + i * KV_WIDTH + g * hd
            return p_ref[:, c0:c0 + hd]
        sl = slice(g * hd, (g + 1) * hd)
        ks_o[:, sl] = _norm_rope(col(2), kg_ref[1:2, :], cos, sin, ones).astype(ks_o.dtype)
        vs_o[:, sl] = col(3).astype(vs_o.dtype)
        kw_o[:, sl] = _norm_rope(col(4), kg_ref[2:3, :], cos, sin, ones).astype(kw_o.dtype)
        vw_o[:, sl] = col(5).astype(vw_o.dtype)
    sig = jax.nn.sigmoid(gp_ref[...])
    per_group = 3 * HEADS_PER_GROUP
    for g in range(N_KV_GROUPS):
        shift = (LANES - g * per_group) % LANES
        g_o[:, g * LANES:(g + 1) * LANES] = sig if shift == 0 else pltpu.roll(sig, shift, 1)


def _nsa_prep(proj, gates_pre, cos, sin, q_g, k_g, seq):
    t = proj.shape[0]
    tm = _tile(seq, 256)
    nseq = seq // tm
    qd = N_HEADS * HEAD_DIM
    kvspec = pl.BlockSpec((tm, KV_WIDTH), lambda i: (i, 0))
    kvshape = jax.ShapeDtypeStruct((t, KV_WIDTH), BF16)
    return pl.pallas_call(
        functools.partial(_nsa_prep_kernel, scale=HEAD_DIM ** -0.5 * LOG2E),
        out_shape=(jax.ShapeDtypeStruct((t, qd), BF16), kvshape, kvshape, kvshape, kvshape,
                   jax.ShapeDtypeStruct((t, N_KV_GROUPS * LANES), F32)),
        grid=(t // tm,),
        in_specs=[
            pl.BlockSpec((tm, proj.shape[1]), lambda i: (i, 0)),
            pl.BlockSpec((tm, LANES), lambda i: (i, 0)),
            pl.BlockSpec((tm, HEAD_DIM), lambda i: (i % nseq, 0)),
            pl.BlockSpec((tm, HEAD_DIM), lambda i: (i % nseq, 0)),
            pl.BlockSpec((1, HEAD_DIM), lambda i: (0, 0)),
            pl.BlockSpec((3, HEAD_DIM), lambda i: (0, 0)),
        ],
        out_specs=(pl.BlockSpec((tm, qd), lambda i: (i, 0)), kvspec, kvspec, kvspec, kvspec,
                   pl.BlockSpec((tm, N_KV_GROUPS * LANES), lambda i: (i, 0))),
        compiler_params=_params("parallel"),
        name="nsa_prep",
    )(proj, gates_pre, cos, sin, q_g.reshape(1, HEAD_DIM), k_g)


def _compress_kernel(kc_ref, vc_ref, w1_ref, w2_ref, pos_ref, kg_ref, cos_ref, sin_ref,
                     ko_ref, vo_ref, *, nchunk):
    def comp(x_ref, i):
        first = jnp.zeros((nchunk, w1_ref.shape[-1]), F32)
        second = jnp.zeros_like(first)
        for l in range(CMP_STRIDE):
            xl = x_ref[pl.ds(l, nchunk, stride=CMP_STRIDE), :]
            xa = (xl + pos_ref[i, l:l + 1, :]).astype(BF16)
            xb = (xl + pos_ref[i, CMP_STRIDE + l:CMP_STRIDE + l + 1, :]).astype(BF16)
            first += jnp.dot(xa, w1_ref[i, l], preferred_element_type=F32)
            second += jnp.dot(xb, w1_ref[i, CMP_STRIDE + l], preferred_element_type=F32)
        pre = first + pltpu.roll(second, nchunk - 1, 0)
        hid = pre * jax.nn.sigmoid(pre)
        return jnp.dot(hid.astype(BF16), w2_ref[i], preferred_element_type=F32)

    k = comp(kc_ref, 0)
    ko_ref[...] = _norm_rope(k, kg_ref[...], cos_ref[...], sin_ref[...]).astype(ko_ref.dtype)
    vo_ref[...] = comp(vc_ref, 1).astype(vo_ref.dtype)


def _compress(proj3, w1, w2, cmp_pos, kg0, cos_c, sin_c):
    batch, seq, _ = proj3.shape
    nchunk = seq // CMP_STRIDE
    hd = HEAD_DIM
    cb = N_HEADS
    out = jax.ShapeDtypeStruct((batch, N_KV_GROUPS, nchunk, hd), BF16)
    ospec = pl.BlockSpec((None, None, nchunk, hd), lambda b, g: (b, g, 0, 0))
    full = lambda a: pl.BlockSpec(a.shape, lambda b, g: (0,) * a.ndim)
    return pl.pallas_call(
        functools.partial(_compress_kernel, nchunk=nchunk),
        out_shape=(out, out),
        grid=(batch, N_KV_GROUPS),
        in_specs=[
            pl.BlockSpec((None, seq, hd), lambda b, g: (b, 0, cb + g)),
            pl.BlockSpec((None, seq, hd), lambda b, g: (b, 0, cb + N_KV_GROUPS + g)),
            full(w1), full(w2), full(cmp_pos), full(kg0), full(cos_c), full(sin_c),
        ],
        out_specs=(ospec, ospec),
        compiler_params=_params("parallel", "parallel"),
        name="nsa_compress",
    )(proj3, proj3, w1, w2, cmp_pos, kg0, cos_c, sin_c)


def _nsa_attn_kernel(q_ref, kc_ref, vc_ref, ks_ref, vs_ref, kw_ref, vw_ref, g_ref, ov_ref, e_ref,
                     o_ref, p_ref, b_ref, m_ref, l_ref, a_ref, acc_ref,
                     *, tq, tk, seq, n_sel, n_top, span):
    J = HEADS_PER_GROUP
    hd = HEAD_DIM
    q0 = pl.program_id(2) * tq
    q = q_ref[...]
    qa = jnp.concatenate([q[:, j * hd:(j + 1) * hd] for j in range(J)], axis=0)
    tpos = q0 + lax.broadcasted_iota(I32, (tq, 1), 0)

    def scores(kblk):
        return lax.dot_general(qa, kblk, (((1,), (1,)), ((), ())), preferred_element_type=F32)

    ncp = kc_ref.shape[0]
    n_idx = lax.broadcasted_iota(I32, (1, ncp), 1)
    mask_c = (n_idx * CMP_STRIDE + (CMP_BLOCK - 1)) <= tpos
    s3 = jnp.where(mask_c[None], scores(kc_ref[...]).reshape(J, tq, ncp), NEG_INF)
    m = jnp.max(s3, axis=-1, keepdims=True)
    p = jnp.where(mask_c[None], jnp.exp2(s3 - m), 0.0)
    l = jnp.sum(p, axis=-1, keepdims=True)
    pc = p * jnp.where(l > 0.0, 1.0 / l, 0.0)
    o_c = jnp.dot(pc.reshape(J * tq, ncp).astype(BF16), vc_ref[...], preferred_element_type=F32)

    nsp = e_ref.shape[1]
    imp = lax.dot_general(ov_ref[...], jnp.sum(pc, axis=0), (((1,), (1,)), ((), ())),
                          preferred_element_type=F32, precision=lax.Precision.HIGHEST)[:nsp]
    blk = lax.broadcasted_iota(I32, (nsp, 1), 0)
    tpos_l = q0 + lax.broadcasted_iota(I32, (1, tq), 1)
    cur = tpos_l // SEL_BLOCK
    forced = (blk == 0) | (blk == cur) | (blk == cur - 1)
    imp = jnp.where(blk * SEL_BLOCK <= tpos_l, imp + FORCE_BONUS * forced.astype(F32), NEG_INF)
    rank = jnp.zeros((nsp, tq), F32)
    for mp in range(n_sel):
        row = imp[mp:mp + 1, :]
        rank += jnp.where(blk > mp, (row >= imp).astype(F32), (row > imp).astype(F32))
    sel = jnp.transpose(((rank < n_top) & (blk < n_sel)).astype(F32)).astype(BF16)

    rb = ATTN_ROW_BLOCK

    def softmax_head(j, kblk, width, online):
        s_head = lax.dot_general(q[:, j * hd:(j + 1) * hd], kblk, (((1,), (1,)), ((), ())),
                                 preferred_element_type=F32)
        lane_tiles = [slice(c, c + LANES) for c in range(0, width, LANES)]
        for h0 in range(0, tq, rb):
            rows = slice(j * tq + h0, j * tq + h0 + rb)
            s = [s_head[h0:h0 + rb, c] + b_ref[h0:h0 + rb, c] for c in lane_tiles]
            m_new = jnp.broadcast_to(jnp.max(functools.reduce(jnp.maximum, s), axis=-1, keepdims=True),
                                     (rb, LANES))
            if online:
                m_old = m_ref[rows, :]
                m_new = jnp.maximum(m_old, m_new)
                alpha = jnp.exp2(m_old - m_new)
                m_ref[rows, :] = m_new
                a_ref[rows, :] = alpha
            p = [jnp.exp2(x - m_new) for x in s]
            row_sum = jnp.broadcast_to(jnp.sum(functools.reduce(jnp.add, p), axis=-1, keepdims=True),
                                       (rb, LANES))
            l_ref[rows, :] = alpha * l_ref[rows, :] + row_sum if online else row_sum
            for c, x in zip(lane_tiles, p):
                p_ref[rows, c] = x.astype(BF16)

    m_ref[...] = jnp.full(m_ref.shape, NEG_INF, F32)
    l_ref[...] = jnp.zeros(l_ref.shape, F32)
    acc_ref[...] = jnp.zeros(acc_ref.shape, F32)

    def sel_chunk(c, carry):
        k0 = pl.multiple_of(c * tk, tk)
        picked = jnp.dot(sel, e_ref[c], preferred_element_type=F32)
        kpos = k0 + lax.broadcasted_iota(I32, (1, tk), 1)
        b_ref[:, :tk] = jnp.where((picked > 0.5) & (kpos <= tpos), 0.0, NEG_INF)
        kblk = ks_ref[pl.ds(k0, tk), :]
        for j in range(J):
            softmax_head(j, kblk, tk, online=True)
        pv = jnp.dot(p_ref[:, :tk], vs_ref[pl.ds(k0, tk), :], preferred_element_type=F32)
        acc_ref[...] = a_ref[...] * acc_ref[...] + pv
        return carry

    lax.fori_loop(0, (q0 + tq + tk - 1) // tk, sel_chunk, 0)
    inv_ls = 1.0 / l_ref[...]

    start = pl.multiple_of(jnp.maximum(q0 + tq - span, 0), tq)
    diff = tpos - (start + lax.broadcasted_iota(I32, (1, span), 1))
    b_ref[:, :span] = jnp.where((diff >= 0) & (diff < WINDOW), 0.0, NEG_INF)
    kblk = kw_ref[pl.ds(start, span), :]
    for j in range(J):
        softmax_head(j, kblk, span, online=False)
    o_w = jnp.dot(p_ref[:, :span], vw_ref[pl.ds(start, span), :], preferred_element_type=F32)
    inv_lw = 1.0 / l_ref[...]

    gt = g_ref[...]
    for j in range(J):
        rows = slice(j * tq, (j + 1) * tq)
        o = (gt[:, 3 * j:3 * j + 1] * o_c[rows]
             + (gt[:, 3 * j + 1:3 * j + 2] * inv_ls[rows]) * acc_ref[rows, :]
             + (gt[:, 3 * j + 2:3 * j + 3] * inv_lw[rows]) * o_w[rows])
        o_ref[:, j * hd:(j + 1) * hd] = o.astype(o_ref.dtype)


def _nsa_attn(qr, kcmp, vcmp, ks, vs, kw, vw, gates, batch, seq):
    hd = HEAD_DIM
    J = HEADS_PER_GROUP
    tq = _tile(seq, TQ)
    tk = _tile(seq, TK_SEL)
    n_sel = seq // SEL_BLOCK
    n_top = min(SEL_TOP_N, n_sel)
    span = min(WINDOW + tq, seq)
    width = max(span, tk)
    ncp = kcmp.shape[2]
    nsp = -(-n_sel // 16) * 16
    ci = jnp.arange(ncp, dtype=I32)[None, :] * CMP_STRIDE
    sj = jnp.arange(LANES, dtype=I32)[:, None] * SEL_BLOCK
    ov = jnp.clip(jnp.minimum(ci + CMP_BLOCK, sj + SEL_BLOCK) - jnp.maximum(ci, sj), 0).astype(F32) / CMP_BLOCK
    ov = jnp.where((jnp.arange(ncp)[None, :] < ncp - 1) & (jnp.arange(LANES)[:, None] < n_sel), ov, 0.0)
    kk = jnp.arange(seq, dtype=I32).reshape(seq // tk, 1, tk)
    expand = (kk // SEL_BLOCK == jnp.arange(nsp, dtype=I32)[None, :, None]).astype(BF16)

    r3 = lambda a: a.reshape(batch, seq, a.shape[-1])
    kv_spec = pl.BlockSpec((None, seq, hd), lambda b, g, i: (b, 0, g))
    cmp_spec = pl.BlockSpec((None, None, ncp, hd), lambda b, g, i: (b, g, 0, 0))
    out = pl.pallas_call(
        functools.partial(_nsa_attn_kernel, tq=tq, tk=tk, seq=seq, n_sel=n_sel, n_top=n_top, span=span),
        out_shape=jax.ShapeDtypeStruct((batch, seq, N_HEADS * hd), BF16),
        grid=(batch, N_KV_GROUPS, seq // tq),
        in_specs=[
            pl.BlockSpec((None, tq, J * hd), lambda b, g, i: (b, i, g)),
            cmp_spec, cmp_spec, kv_spec, kv_spec, kv_spec, kv_spec,
            pl.BlockSpec((None, tq, LANES), lambda b, g, i: (b, i, g)),
            pl.BlockSpec(ov.shape, lambda b, g, i: (0, 0)),
            pl.BlockSpec(expand.shape, lambda b, g, i: (0, 0, 0)),
        ],
        out_specs=pl.BlockSpec((None, tq, J * hd), lambda b, g, i: (b, i, g)),
        scratch_shapes=[
            pltpu.VMEM((J * tq, width), BF16),
            pltpu.VMEM((tq, width), F32),
            pltpu.VMEM((J * tq, LANES), F32),
            pltpu.VMEM((J * tq, LANES), F32),
            pltpu.VMEM((J * tq, LANES), F32),
            pltpu.VMEM((J * tq, hd), F32),
        ],
        compiler_params=_params("parallel", "parallel", "parallel"),
        name="nsa_attn",
    )(r3(qr), kcmp, vcmp, r3(ks), r3(vs), r3(kw), r3(vw), r3(gates), ov, expand)
    return out.reshape(batch * seq, N_HEADS * hd)


def _route_kernel(lg_ref, tri_ref, mi_ref, mw_ref, cnt_ref, carry_ref):
    @pl.when(pl.program_id(0) == 0)
    def _():
        carry_ref[...] = jnp.zeros_like(carry_ref)

    lane = lax.broadcasted_iota(I32, lg_ref.shape, 1)
    lg = jnp.where(lane < N_EXPERTS, lg_ref[...], -jnp.inf)
    m1 = jnp.max(lg, axis=-1, keepdims=True)
    i1 = jnp.min(jnp.where(lg == m1, lane, LANES), axis=-1, keepdims=True)
    lg2 = jnp.where(lane == i1, -jnp.inf, lg)
    m2 = jnp.max(lg2, axis=-1, keepdims=True)
    i2 = jnp.min(jnp.where(lg2 == m2, lane, LANES), axis=-1, keepdims=True)
    e2 = jnp.exp(m2 - m1)
    w1 = 1.0 / (1.0 + e2)
    w2 = e2 / (1.0 + e2)
    oh1, oh2 = lane == i1, lane == i2
    onehot = (oh1 | oh2).astype(F32)
    before = jnp.dot(tri_ref[...], onehot.astype(BF16), preferred_element_type=F32) + carry_ref[...]
    pos1 = jnp.sum(jnp.where(oh1, before, 0.0), axis=-1, keepdims=True).astype(I32)
    pos2 = jnp.sum(jnp.where(oh2, before, 0.0), axis=-1, keepdims=True).astype(I32)
    carry_ref[...] += jnp.sum(onehot, axis=0, keepdims=True)
    mi_ref[...] = jnp.where(lane == 0, i1, jnp.where(lane == 1, i2,
                            jnp.where(lane == 2, pos1, jnp.where(lane == 3, pos2, 0))))
    mw_ref[...] = jnp.where(lane == 0, w1, jnp.where(lane == 1, w2, 0.0))
    cnt_ref[...] = carry_ref[...]


def _route(logits):
    t = logits.shape[0]
    tm = _tile(t, 512)
    tri = (jnp.arange(tm)[None, :] < jnp.arange(tm)[:, None]).astype(BF16)
    return pl.pallas_call(
        _route_kernel,
        out_shape=(jax.ShapeDtypeStruct((t, LANES), I32), jax.ShapeDtypeStruct((t, LANES), F32),
                   jax.ShapeDtypeStruct((1, LANES), F32)),
        grid=(t // tm,),
        in_specs=[pl.BlockSpec((tm, LANES), lambda i: (i, 0)), pl.BlockSpec((tm, tm), lambda i: (0, 0))],
        out_specs=(pl.BlockSpec((tm, LANES), lambda i: (i, 0)), pl.BlockSpec((tm, LANES), lambda i: (i, 0)),
                   pl.BlockSpec((1, LANES), lambda i: (0, 0))),
        scratch_shapes=[pltpu.VMEM((1, LANES), F32)],
        compiler_params=_params("arbitrary"),
        name="moe_route",
    )(logits, tri)


def _row_copy(src_hbm, row, buf, slot, r, sem):
    return pltpu.make_async_copy(src_hbm.at[pl.ds(row, 1), :], buf.at[slot, pl.ds(r, 1), :], sem.at[slot])


def _issue_rows(idx_ref, src_hbm, buf, slot, base, sem, rows):
    def body(r, carry):
        _row_copy(src_hbm, idx_ref[0, r], buf, slot, base + r, sem).start()
        return carry

    lax.fori_loop(0, rows, body, 0, unroll=8)


def _wait_rows(src_hbm, buf, slot, sem, rows):
    def body(r, carry):
        _row_copy(src_hbm, 0, buf, slot, r, sem).wait()
        return carry

    lax.fori_loop(0, rows, body, 0, unroll=8)


def _dispatch_kernel(idx_ref, nxt_ref, src_hbm, o_ref, buf, sem, *, rows):
    i = pl.program_id(0)
    slot = i % 2

    @pl.when(i == 0)
    def _():
        _issue_rows(idx_ref, src_hbm, buf, 0, 0, sem, rows)

    @pl.when(i + 1 < pl.num_programs(0))
    def _():
        _issue_rows(nxt_ref, src_hbm, buf, 1 - slot, 0, sem, rows)

    _wait_rows(src_hbm, buf, slot, sem, rows)
    o_ref[...] = buf[slot].astype(o_ref.dtype)


def _dispatch_rows(src, idx):
    n = idx.shape[0]
    d = src.shape[1]
    tg = _tile(n, TG)
    steps = n // tg
    smem = lambda f: pl.BlockSpec((None, 1, tg), f, memory_space=pltpu.SMEM)
    idx3 = idx.reshape(steps, 1, tg)
    return pl.pallas_call(
        functools.partial(_dispatch_kernel, rows=tg),
        out_shape=jax.ShapeDtypeStruct((n, d), BF16),
        grid=(steps,),
        in_specs=[
            smem(lambda i: (i, 0, 0)),
            smem(lambda i: (jnp.minimum(i + 1, steps - 1), 0, 0)),
            pl.BlockSpec(memory_space=pl.ANY),
        ],
        out_specs=pl.BlockSpec((tg, d), lambda i: (i, 0)),
        scratch_shapes=[pltpu.VMEM((2, tg, d), F32), pltpu.SemaphoreType.DMA((2,))],
        compiler_params=_params("arbitrary"),
        name="moe_dispatch",
    )(idx3, idx3, src)


def _combine_kernel(a_ref, b_ref, na_ref, nb_ref, y_hbm, x_ref, g_ref, mw_ref, o_ref, buf, sem, *, rows):
    i = pl.program_id(0)
    slot = i % 2

    def issue(first_ref, second_ref, s):
        _issue_rows(first_ref, y_hbm, buf, s, 0, sem, rows)
        _issue_rows(second_ref, y_hbm, buf, s, rows, sem, rows)

    @pl.when(i == 0)
    def _():
        issue(a_ref, b_ref, 0)

    @pl.when(i + 1 < pl.num_programs(0))
    def _():
        issue(na_ref, nb_ref, 1 - slot)

    _wait_rows(y_hbm, buf, slot, sem, 2 * rows)
    mw = mw_ref[...]
    f = mw[:, 0:1] * buf[slot, 0:rows, :] + mw[:, 1:2] * buf[slot, rows:2 * rows, :]
    o_ref[...] = x_ref[...] + g_ref[...] * f


def _combine_rows(ys, d1, d2, meta_w, x, gate, seq):
    t, d = x.shape
    b = gate.shape[0]
    tc = _tile(seq, TCMB)
    steps = t // tc
    smem = lambda f: pl.BlockSpec((None, 1, tc), f, memory_space=pltpu.SMEM)
    cur, nxt = (lambda i: (i, 0, 0)), (lambda i: (jnp.minimum(i + 1, steps - 1), 0, 0))
    a3, b3 = d1.reshape(steps, 1, tc), d2.reshape(steps, 1, tc)
    return pl.pallas_call(
        functools.partial(_combine_kernel, rows=tc),
        out_shape=jax.ShapeDtypeStruct((t, d), F32),
        grid=(steps,),
        in_specs=[
            smem(cur), smem(cur), smem(nxt), smem(nxt),
            pl.BlockSpec(memory_space=pl.ANY),
            pl.BlockSpec((tc, d), lambda i: (i, 0)),
            pl.BlockSpec((None, 1, d), lambda i: (i * tc // seq, 0, 0)),
            pl.BlockSpec((tc, LANES), lambda i: (i, 0)),
        ],
        out_specs=pl.BlockSpec((tc, d), lambda i: (i, 0)),
        scratch_shapes=[pltpu.VMEM((2, 2 * tc, d), F32), pltpu.SemaphoreType.DMA((2,))],
        compiler_params=_params("arbitrary"),
        name="moe_combine",
    )(a3, b3, a3, b3, ys, x, gate.reshape(b, 1, d), meta_w)


def _moe_ffn_residual(x, hf, logits, gate, w_gate, w_up, w_down, seq):
    t, d = hf.shape
    n_exp = w_gate.shape[0]
    tm = _tile(t, TM_MOE)
    meta_i, meta_w, counts = _route(logits)
    e1, e2, p1, p2 = meta_i[:, 0], meta_i[:, 1], meta_i[:, 2], meta_i[:, 3]
    cnt = counts[0, :n_exp].astype(I32)
    padded = (cnt + tm - 1) // tm * tm
    ends = jnp.cumsum(padded)
    starts = ends - padded
    d1, d2 = starts[e1] + p1, starts[e2] + p2
    n_rows = 2 * t + n_exp * tm
    tok = jnp.arange(t, dtype=I32)
    src = jnp.zeros((n_rows,), I32).at[d1].set(tok).at[d2].set(tok)
    tiles = jnp.arange(n_rows // tm, dtype=I32)
    tile_expert = jnp.minimum(jnp.sum((tiles * tm)[:, None] >= ends[None, :], axis=1), n_exp - 1).astype(I32)
    tile_row = jnp.minimum(tiles, ends[-1] // tm - 1)
    tile_first = jnp.concatenate([jnp.ones((1,), I32), (tile_expert[1:] != tile_expert[:-1]).astype(I32)])
    sched = (tile_expert, tile_row, tile_first)

    xs = _dispatch_rows(hf, src)
    hs = _ws_mm(_ws_glu_kernel, xs, (w_gate, w_up), sched, tm, TN_MOE, BF16, "moe_glu")
    ys = _ws_mm(_ws_mm_kernel, hs, (w_down,), sched, tm, TN_MOE_DOWN, F32, "moe_down")
    return _combine_rows(ys, d1, d2, meta_w, x, gate, seq)


def kernel(x, c, ada_w, ada_b, norm_mix_g, norm_ffn_g, conv_w_in, conv_w, conv_w_out, nsa_w_in, nsa_w_o,
           nsa_q_norm_g, nsa_k_norm_g, cmp_pos, cmp_w1, cmp_w2, ffn_w_gate, ffn_w_up, ffn_w_down,
           router_w, router_b, moe_w_gate, moe_w_up, moe_w_down):
    batch, seq, d = x.shape
    t = batch * seq
    bf = lambda a: a.astype(BF16)

    mod = _ada_mod(c, ada_w, ada_b)
    sh_m, sc_m, g_m, sh_f, sc_f, g_f = [mod[:, :, i * d:(i + 1) * d] for i in range(6)]
    xt = x.reshape(t, d)

    hm = _norm_mod(xt, norm_mix_g[0], sc_m[0], sh_m[0], seq)
    bcu = _mm(hm, bf(conv_w_in[0]), BF16, "conv_in")
    z = _conv_gate(bcu, conv_w[0], batch, seq)
    xt = _mm_res(z, bf(conv_w_out[0]), xt, g_m[0], seq, "conv_out")
    hf = _norm_mod(xt, norm_ffn_g[0], sc_f[0], sh_f[0], seq)
    tm_ffn = _tile(t, 2 * TM)
    hid = _ws_mm(_ws_glu_kernel, hf, (ffn_w_gate[0:1], ffn_w_up[0:1]), _dense_sched(t // tm_ffn), tm_ffn,
                 TN_GLU, BF16, "ffn_glu")
    xt = _mm_res(hid, bf(ffn_w_down[0]), xt, g_f[0], seq, "ffn_down", tm=TM_DOWN, tn=TN_DOWN)

    hm = _norm_mod(xt, norm_mix_g[1], sc_m[1], sh_m[1], seq)
    qkv_w = N_HEADS * HEAD_DIM + 6 * KV_WIDTH
    w_in = nsa_w_in[0]
    proj = _mm(hm, bf(w_in[:, :qkv_w]), F32, "nsa_in")
    w_gates = jnp.pad(w_in[:, qkv_w:], ((0, 0), (0, LANES - 3 * N_HEADS)))
    gates_pre = _mm(hm, bf(w_gates), F32, "nsa_gates")
    pos = jnp.arange(seq, dtype=I32)
    cos, sin = _rope_tables(pos)
    qr, ks, vs, kw, vw, gates = _nsa_prep(proj, gates_pre, cos, sin, nsa_q_norm_g[0], nsa_k_norm_g[0], seq)
    cmp_end = jnp.arange(seq // CMP_STRIDE, dtype=I32) * CMP_STRIDE + CMP_BLOCK - 1
    cos_c, sin_c = _rope_tables(cmp_end)
    kcmp, vcmp = _compress(proj.reshape(batch, seq, qkv_w), bf(cmp_w1[0]), bf(cmp_w2[0]), cmp_pos[0],
                           nsa_k_norm_g[0, 0:1], cos_c, sin_c)
    attn = _nsa_attn(qr, kcmp, vcmp, ks, vs, kw, vw, gates, batch, seq)
    xt = _mm_res(attn, bf(nsa_w_o[0]), xt, g_m[1], seq, "nsa_out")

    n_exp = router_w.shape[2]
    rw = jnp.pad(router_w[0], ((0, 0), (0, LANES - n_exp)))
    rb = jnp.pad(router_b[0], (0, LANES - n_exp)).reshape(1, LANES)
    hf, logits = _norm_mod(xt, norm_ffn_g[1], sc_f[1], sh_f[1], seq, router=(rw, rb))
    out = _moe_ffn_residual(xt, hf, logits, g_f[1], moe_w_gate[0], moe_w_up[0], moe_w_down[0], seq)
    return out.reshape(batch, seq, d)
```

```python
import functools

import jax
import jax.numpy as jnp
from jax import lax
from jax.experimental import pallas as pl
from jax.experimental.pallas import tpu as pltpu

F32 = jnp.float32
BF16 = jnp.bfloat16
I32 = jnp.int32

EPS = 1e-6
NEG_INF = -1e30
FORCE_BONUS = 1e4
ROPE_THETA = 10000.0
LOG2E = 1.4426950408889634

CONV_WIDTH = 3
N_HEADS = 32
HEAD_DIM = 128
N_KV_GROUPS = 4
HEADS_PER_GROUP = N_HEADS // N_KV_GROUPS
KV_WIDTH = N_KV_GROUPS * HEAD_DIM
CMP_BLOCK = 32
CMP_STRIDE = 16
SEL_BLOCK = 64
SEL_TOP_N = 16
WINDOW = 512
N_EXPERTS = 8

LANES = 128
VMEM_LIMIT_V7X = 60 * 1024 * 1024

TM = 1024
TN = 1024
TN_RES = 1024
TN_GLU = 256
TM_DOWN = 512
TN_DOWN = 512
TM_ROW = 512
TC_CONV = 256
TQ = 256
TK_SEL = 512
ATTN_ROW_BLOCK = 64
TM_MOE = 512
TN_MOE = 512
TN_MOE_DOWN = 1024
TG = 256
TCMB = 128


def _params(*sem):
    return pltpu.CompilerParams(dimension_semantics=sem, vmem_limit_bytes=VMEM_LIMIT_V7X)


def _tile(full, want):
    return want if full % want == 0 else full


def _ada_kernel(c_ref, w_ref, b_ref, o_ref):
    c = c_ref[...]
    cond = c * jax.nn.sigmoid(c)
    acc = jnp.dot(cond.astype(BF16), w_ref[...].astype(BF16), preferred_element_type=F32)
    o_ref[...] = acc + b_ref[...]


def _ada_mod(c, ada_w, ada_b):
    depth, d, n = ada_w.shape
    b = c.shape[0]
    tn = _tile(n, 512)
    return pl.pallas_call(
        _ada_kernel,
        out_shape=jax.ShapeDtypeStruct((depth, b, n), F32),
        grid=(depth, n // tn),
        in_specs=[
            pl.BlockSpec((b, d), lambda l, j: (0, 0)),
            pl.BlockSpec((None, d, tn), lambda l, j: (l, 0, j)),
            pl.BlockSpec((None, 1, tn), lambda l, j: (l, 0, j)),
        ],
        out_specs=pl.BlockSpec((None, b, tn), lambda l, j: (l, 0, j)),
        compiler_params=_params("parallel", "parallel"),
        name="ada_mod",
    )(c, ada_w, ada_b.reshape(depth, 1, n))


def _norm_mod_value(x_ref, g_ref, sc_ref, sh_ref):
    x = x_ref[...]
    ms = jnp.mean(x * x, axis=-1, keepdims=True)
    y = x * lax.rsqrt(ms + EPS) * g_ref[...]
    return y * (1.0 + sc_ref[...]) + sh_ref[...]


def _norm_mod_kernel(x_ref, g_ref, sc_ref, sh_ref, o_ref):
    o_ref[...] = _norm_mod_value(x_ref, g_ref, sc_ref, sh_ref).astype(o_ref.dtype)


def _norm_mod_router_kernel(x_ref, g_ref, sc_ref, sh_ref, rw_ref, rb_ref, o_ref, lg_ref):
    h = _norm_mod_value(x_ref, g_ref, sc_ref, sh_ref)
    o_ref[...] = h.astype(o_ref.dtype)
    lg_ref[...] = jnp.dot(h, rw_ref[...], preferred_element_type=F32,
                          precision=lax.Precision.HIGHEST) + rb_ref[...]


def _norm_mod(x, g, sc, sh, seq, router=None):
    t, d = x.shape
    b = sc.shape[0]
    tm = _tile(seq, TM_ROW)
    row_specs = [
        pl.BlockSpec((tm, d), lambda i: (i, 0)),
        pl.BlockSpec((1, d), lambda i: (0, 0)),
        pl.BlockSpec((None, 1, d), lambda i: (i * tm // seq, 0, 0)),
        pl.BlockSpec((None, 1, d), lambda i: (i * tm // seq, 0, 0)),
    ]
    args = [x, g.reshape(1, d), sc.reshape(b, 1, d), sh.reshape(b, 1, d)]
    if router is None:
        return pl.pallas_call(
            _norm_mod_kernel,
            out_shape=jax.ShapeDtypeStruct((t, d), BF16),
            grid=(t // tm,),
            in_specs=row_specs,
            out_specs=pl.BlockSpec((tm, d), lambda i: (i, 0)),
            compiler_params=_params("parallel"),
            name="norm_mod",
        )(*args)
    rw, rb = router
    return pl.pallas_call(
        _norm_mod_router_kernel,
        out_shape=(jax.ShapeDtypeStruct((t, d), F32), jax.ShapeDtypeStruct((t, LANES), F32)),
        grid=(t // tm,),
        in_specs=row_specs + [
            pl.BlockSpec((d, LANES), lambda i: (0, 0)),
            pl.BlockSpec((1, LANES), lambda i: (0, 0)),
        ],
        out_specs=(pl.BlockSpec((tm, d), lambda i: (i, 0)), pl.BlockSpec((tm, LANES), lambda i: (i, 0))),
        compiler_params=_params("parallel"),
        name="norm_mod_router",
    )(*args, rw, rb)


def _mm_kernel(x_ref, w_ref, o_ref):
    o_ref[...] = jnp.dot(x_ref[...], w_ref[...], preferred_element_type=F32).astype(o_ref.dtype)


def _mm(x, w, out_dtype, name, tm=TM, tn=TN):
    m, k = x.shape
    n = w.shape[1]
    tm, tn = _tile(m, tm), _tile(n, tn)
    return pl.pallas_call(
        _mm_kernel,
        out_shape=jax.ShapeDtypeStruct((m, n), out_dtype),
        grid=(m // tm, n // tn),
        in_specs=[pl.BlockSpec((tm, k), lambda i, j: (i, 0)), pl.BlockSpec((k, tn), lambda i, j: (0, j))],
        out_specs=pl.BlockSpec((tm, tn), lambda i, j: (i, j)),
        compiler_params=_params("parallel", "parallel"),
        name=name,
    )(x, w)


def _mm_res_kernel(x_ref, w_ref, res_ref, g_ref, o_ref):
    o_ref[...] = res_ref[...] + g_ref[...] * jnp.dot(x_ref[...], w_ref[...], preferred_element_type=F32)


def _mm_res(x, w, res, gate, seq, name, tm=TM, tn=TN_RES):
    m, k = x.shape
    n = w.shape[1]
    b = gate.shape[0]
    tm, tn = _tile(seq, tm), _tile(n, tn)
    return pl.pallas_call(
        _mm_res_kernel,
        out_shape=jax.ShapeDtypeStruct((m, n), F32),
        grid=(m // tm, n // tn),
        in_specs=[
            pl.BlockSpec((tm, k), lambda i, j: (i, 0)),
            pl.BlockSpec((k, tn), lambda i, j: (0, j)),
            pl.BlockSpec((tm, tn), lambda i, j: (i, j)),
            pl.BlockSpec((None, 1, tn), lambda i, j: (i * tm // seq, 0, j)),
        ],
        out_specs=pl.BlockSpec((tm, tn), lambda i, j: (i, j)),
        compiler_params=_params("parallel", "parallel"),
        name=name,
    )(x, w, res, gate.reshape(b, 1, n))


def _silu_mul(a, b):
    return (a * jax.nn.sigmoid(a)) * b


def _ws_glu_kernel(te_ref, tr_ref, tf_ref, x_ref, wg_ref, wu_ref, o_ref, wg_bf, wu_bf):
    i = pl.program_id(1)

    @pl.when(tf_ref[i] == 1)
    def _():
        wg_bf[...] = wg_ref[...].astype(BF16)
        wu_bf[...] = wu_ref[...].astype(BF16)

    @pl.when(tr_ref[i] == i)
    def _():
        x = x_ref[...]
        a = jnp.dot(x, wg_bf[...], preferred_element_type=F32)
        b = jnp.dot(x, wu_bf[...], preferred_element_type=F32)
        o_ref[...] = _silu_mul(a, b).astype(o_ref.dtype)

    @pl.when(tr_ref[i] != i)
    def _():
        o_ref[...] = jnp.zeros_like(o_ref)


def _ws_mm_kernel(te_ref, tr_ref, tf_ref, x_ref, w_ref, o_ref, w_bf):
    i = pl.program_id(1)

    @pl.when(tf_ref[i] == 1)
    def _():
        w_bf[...] = w_ref[...].astype(BF16)

    @pl.when(tr_ref[i] == i)
    def _():
        o_ref[...] = jnp.dot(x_ref[...], w_bf[...], preferred_element_type=F32).astype(o_ref.dtype)

    @pl.when(tr_ref[i] != i)
    def _():
        o_ref[...] = jnp.zeros_like(o_ref)


def _ws_mm(kernel, xs, ws, sched, tm, tn, out_dtype, name):
    r, k = xs.shape
    n = ws[0].shape[2]
    tn = _tile(n, tn)
    w_spec = pl.BlockSpec((None, k, tn), lambda j, i, te, tr, tf: (te[i], 0, j))
    return pl.pallas_call(
        kernel,
        out_shape=jax.ShapeDtypeStruct((r, n), out_dtype),
        grid_spec=pltpu.PrefetchScalarGridSpec(
            num_scalar_prefetch=3,
            grid=(n // tn, r // tm),
            in_specs=[pl.BlockSpec((tm, k), lambda j, i, te, tr, tf: (tr[i], 0))] + [w_spec] * len(ws),
            out_specs=pl.BlockSpec((tm, tn), lambda j, i, te, tr, tf: (i, j)),
            scratch_shapes=[pltpu.VMEM((k, tn), BF16)] * len(ws),
        ),
        compiler_params=_params("parallel", "arbitrary"),
        name=name,
    )(*sched, xs, *ws)


def _dense_sched(n_tiles):
    tiles = jnp.arange(n_tiles, dtype=I32)
    return jnp.zeros((n_tiles,), I32), tiles, (tiles == 0).astype(I32)


SUBLANES = 8


def _conv_mix_kernel(x_ref, wb_ref, wc_ref, wu_ref, cw_ref, o_ref, tail_ref):
    @pl.when(pl.program_id(2) == 0)
    def _():
        tail_ref[...] = jnp.zeros_like(tail_ref)

    x = x_ref[...]
    v = (jnp.dot(x, wc_ref[...], preferred_element_type=F32)
         * jnp.dot(x, wu_ref[...], preferred_element_type=F32))
    prev = tail_ref[...]
    tail_ref[...] = v[v.shape[0] - SUBLANES:, :]
    rows = lax.broadcasted_iota(I32, v.shape, 0)
    rest = v[SUBLANES:, :]
    v1 = jnp.where(rows >= 1, pltpu.roll(v, 1, 0), jnp.concatenate([pltpu.roll(prev, 1, 0), rest], axis=0))
    v2 = jnp.where(rows >= 2, pltpu.roll(v, 2, 0), jnp.concatenate([pltpu.roll(prev, 2, 0), rest], axis=0))
    w = cw_ref[...]
    y = w[0:1, :] * v2 + w[1:2, :] * v1 + w[2:3, :] * v
    o_ref[...] = (jnp.dot(x, wb_ref[...], preferred_element_type=F32) * y).astype(o_ref.dtype)


def _conv_mix(h, w_in, conv_w, batch, seq):
    t, k = h.shape
    d = conv_w.shape[1]
    tm = _tile(seq, TM)
    tc = _tile(d, TC_CONV)
    nc, nh = d // tc, seq // tm
    w_spec = lambda part: pl.BlockSpec((k, tc), lambda b, j, r: (0, part * nc + j))
    return pl.pallas_call(
        _conv_mix_kernel,
        out_shape=jax.ShapeDtypeStruct((t, d), BF16),
        grid=(batch, nc, nh),
        in_specs=[
            pl.BlockSpec((tm, k), lambda b, j, r: (b * nh + r, 0)),
            w_spec(0), w_spec(1), w_spec(2),
            pl.BlockSpec((CONV_WIDTH, tc), lambda b, j, r: (0, j)),
        ],
        out_specs=pl.BlockSpec((tm, tc), lambda b, j, r: (b * nh + r, j)),
        scratch_shapes=[pltpu.VMEM((SUBLANES, tc), F32)],
        compiler_params=_params("parallel", "parallel", "arbitrary"),
        name="conv_mix",
    )(h, w_in, w_in, w_in, conv_w)


def _rope_tables(pos):
    half = HEAD_DIM // 2
    inv = ROPE_THETA ** (-jnp.arange(half, dtype=F32) / half)
    ang = pos.astype(F32)[:, None] * inv[None, :]
    cos, sin = jnp.cos(ang), jnp.sin(ang)
    return jnp.concatenate([cos, cos], axis=-1), jnp.concatenate([-sin, sin], axis=-1)


def _norm_rope(x, g, cos, sin_signed, ones=None):
    if ones is None:
        ms = jnp.mean(x * x, axis=-1, keepdims=True)
    else:
        ms = jnp.dot((x * x).astype(BF16), ones, preferred_element_type=F32) * (1.0 / HEAD_DIM)
    xn = x * lax.rsqrt(ms + EPS) * g
    return xn * cos + pltpu.roll(xn, HEAD_DIM // 2, 1) * sin_signed


def _nsa_prep_kernel(p_ref, gp_ref, cos_ref, sin_ref, qg_ref, kg_ref,
                     q_o, ks_o, vs_o, kw_o, vw_o, g_o, *, scale):
    cos, sin = cos_ref[...], sin_ref[...]
    hd = HEAD_DIM
    qg = qg_ref[...]
    ones = jnp.ones((hd, hd), BF16)
    for h in range(N_HEADS):
        x = p_ref[:, h * hd:(h + 1) * hd]
        q_o[:, h * hd:(h + 1) * hd] = (_norm_rope(x, qg, cos, sin, ones) * scale).astype(q_o.dtype)
    kv0 = N_HEADS * hd
    for g in range(N_KV_GROUPS):
        def col(i, g=g):
            c0 = kv0 + i * KV_WIDTH + g * hd
            return p_ref[:, c0:c0 + hd]
        sl = slice(g * hd, (g + 1) * hd)
        ks_o[:, sl] = _norm_rope(col(2), kg_ref[1:2, :], cos, sin, ones).astype(ks_o.dtype)
        vs_o[:, sl] = col(3).astype(vs_o.dtype)
        kw_o[:, sl] = _norm_rope(col(4), kg_ref[2:3, :], cos, sin, ones).astype(kw_o.dtype)
        vw_o[:, sl] = col(5).astype(vw_o.dtype)
    sig = jax.nn.sigmoid(gp_ref[...])
    per_group = 3 * HEADS_PER_GROUP
    for g in range(N_KV_GROUPS):
        shift = (LANES - g * per_group) % LANES
        g_o[:, g * LANES:(g + 1) * LANES] = sig if shift == 0 else pltpu.roll(sig, shift, 1)


def _nsa_prep(proj, gates_pre, cos, sin, q_g, k_g, seq):
    t = proj.shape[0]
    tm = _tile(seq, 256)
    nseq = seq // tm
    qd = N_HEADS * HEAD_DIM
    kvspec = pl.BlockSpec((tm, KV_WIDTH), lambda i: (i, 0))
    kvshape = jax.ShapeDtypeStruct((t, KV_WIDTH), BF16)
    return pl.pallas_call(
        functools.partial(_nsa_prep_kernel, scale=HEAD_DIM ** -0.5 * LOG2E),
        out_shape=(jax.ShapeDtypeStruct((t, qd), BF16), kvshape, kvshape, kvshape, kvshape,
                   jax.ShapeDtypeStruct((t, N_KV_GROUPS * LANES), F32)),
        grid=(t // tm,),
        in_specs=[
            pl.BlockSpec((tm, proj.shape[1]), lambda i: (i, 0)),
            pl.BlockSpec((tm, LANES), lambda i: (i, 0)),
            pl.BlockSpec((tm, HEAD_DIM), lambda i: (i % nseq, 0)),
            pl.BlockSpec((tm, HEAD_DIM), lambda i: (i % nseq, 0)),
            pl.BlockSpec((1, HEAD_DIM), lambda i: (0, 0)),
            pl.BlockSpec((3, HEAD_DIM), lambda i: (0, 0)),
        ],
        out_specs=(pl.BlockSpec((tm, qd), lambda i: (i, 0)), kvspec, kvspec, kvspec, kvspec,
                   pl.BlockSpec((tm, N_KV_GROUPS * LANES), lambda i: (i, 0))),
        compiler_params=_params("parallel"),
        name="nsa_prep",
    )(proj, gates_pre, cos, sin, q_g.reshape(1, HEAD_DIM), k_g)


def _compress_kernel(kc_ref, vc_ref, w1_ref, w2_ref, pos_ref, kg_ref, cos_ref, sin_ref,
                     ko_ref, vo_ref, *, nchunk):
    def comp(x_ref, i):
        first = jnp.zeros((nchunk, w1_ref.shape[-1]), F32)
        second = jnp.zeros_like(first)
        for l in range(CMP_STRIDE):
            xl = x_ref[pl.ds(l, nchunk, stride=CMP_STRIDE), :]
            xa = (xl + pos_ref[i, l:l + 1, :]).astype(BF16)
            xb = (xl + pos_ref[i, CMP_STRIDE + l:CMP_STRIDE + l + 1, :]).astype(BF16)
            first += jnp.dot(xa, w1_ref[i, l], preferred_element_type=F32)
            second += jnp.dot(xb, w1_ref[i, CMP_STRIDE + l], preferred_element_type=F32)
        pre = first + pltpu.roll(second, nchunk - 1, 0)
        hid = pre * jax.nn.sigmoid(pre)
        return jnp.dot(hid.astype(BF16), w2_ref[i], preferred_element_type=F32)

    k = comp(kc_ref, 0)
    ko_ref[...] = _norm_rope(k, kg_ref[...], cos_ref[...], sin_ref[...]).astype(ko_ref.dtype)
    vo_ref[...] = comp(vc_ref, 1).astype(vo_ref.dtype)


def _compress(proj3, w1, w2, cmp_pos, kg0, cos_c, sin_c):
    batch, seq, _ = proj3.shape
    nchunk = seq // CMP_STRIDE
    hd = HEAD_DIM
    cb = N_HEADS
    out = jax.ShapeDtypeStruct((batch, N_KV_GROUPS, nchunk, hd), BF16)
    ospec = pl.BlockSpec((None, None, nchunk, hd), lambda b, g: (b, g, 0, 0))
    full = lambda a: pl.BlockSpec(a.shape, lambda b, g: (0,) * a.ndim)
    return pl.pallas_call(
        functools.partial(_compress_kernel, nchunk=nchunk),
        out_shape=(out, out),
        grid=(batch, N_KV_GROUPS),
        in_specs=[
            pl.BlockSpec((None, seq, hd), lambda b, g: (b, 0, cb + g)),
            pl.BlockSpec((None, seq, hd), lambda b, g: (b, 0, cb + N_KV_GROUPS + g)),
            full(w1), full(w2), full(cmp_pos), full(kg0), full(cos_c), full(sin_c),
        ],
        out_specs=(ospec, ospec),
        compiler_params=_params("parallel", "parallel"),
        name="nsa_compress",
    )(proj3, proj3, w1, w2, cmp_pos, kg0, cos_c, sin_c)


def _nsa_attn_kernel(q_ref, kc_ref, vc_ref, ks_ref, vs_ref, kw_ref, vw_ref, g_ref, ov_ref, e_ref,
                     o_ref, p_ref, b_ref, m_ref, l_ref, a_ref, acc_ref,
                     *, tq, tk, seq, n_sel, n_top, span):
    J = HEADS_PER_GROUP
    hd = HEAD_DIM
    q0 = pl.program_id(2) * tq
    q = q_ref[...]
    qa = jnp.concatenate([q[:, j * hd:(j + 1) * hd] for j in range(J)], axis=0)
    tpos = q0 + lax.broadcasted_iota(I32, (tq, 1), 0)

    def scores(kblk):
        return lax.dot_general(qa, kblk, (((1,), (1,)), ((), ())), preferred_element_type=F32)

    ncp = kc_ref.shape[0]
    n_idx = lax.broadcasted_iota(I32, (1, ncp), 1)
    mask_c = (n_idx * CMP_STRIDE + (CMP_BLOCK - 1)) <= tpos
    s3 = jnp.where(mask_c[None], scores(kc_ref[...]).reshape(J, tq, ncp), NEG_INF)
    m = jnp.max(s3, axis=-1, keepdims=True)
    p = jnp.where(mask_c[None], jnp.exp2(s3 - m), 0.0)
    l = jnp.sum(p, axis=-1, keepdims=True)
    pc = p * jnp.where(l > 0.0, 1.0 / l, 0.0)
    o_c = jnp.dot(pc.reshape(J * tq, ncp).astype(BF16), vc_ref[...], preferred_element_type=F32)

    nsp = e_ref.shape[1]
    imp = lax.dot_general(ov_ref[...], jnp.sum(pc, axis=0), (((1,), (1,)), ((), ())),
                          preferred_element_type=F32, precision=lax.Precision.HIGHEST)[:nsp]
    blk = lax.broadcasted_iota(I32, (nsp, 1), 0)
    tpos_l = q0 + lax.broadcasted_iota(I32, (1, tq), 1)
    cur = tpos_l // SEL_BLOCK
    forced = (blk == 0) | (blk == cur) | (blk == cur - 1)
    imp = jnp.where(blk * SEL_BLOCK <= tpos_l, imp + FORCE_BONUS * forced.astype(F32), NEG_INF)
    rank = jnp.zeros((nsp, tq), F32)
    for mp in range(n_sel):
        row = imp[mp:mp + 1, :]
        rank += jnp.where(blk > mp, (row >= imp).astype(F32), (row > imp).astype(F32))
    sel = jnp.transpose(((rank < n_top) & (blk < n_sel)).astype(F32)).astype(BF16)

    rb = ATTN_ROW_BLOCK

    def softmax_head(j, kblk, width, online):
        s_head = lax.dot_general(q[:, j * hd:(j + 1) * hd], kblk, (((1,), (1,)), ((), ())),
                                 preferred_element_type=F32)
        lane_tiles = [slice(c, c + LANES) for c in range(0, width, LANES)]
        for h0 in range(0, tq, rb):
            rows = slice(j * tq + h0, j * tq + h0 + rb)
            s = [s_head[h0:h0 + rb, c] + b_ref[h0:h0 + rb, c] for c in lane_tiles]
            m_new = jnp.broadcast_to(jnp.max(functools.reduce(jnp.maximum, s), axis=-1, keepdims=True),
                                     (rb, LANES))
            if online:
                m_old = m_ref[rows, :]
                m_new = jnp.maximum(m_old, m_new)
                alpha = jnp.exp2(m_old - m_new)
                m_ref[rows, :] = m_new
                a_ref[rows, :] = alpha
            p = [jnp.exp2(x - m_new) for x in s]
            row_sum = jnp.broadcast_to(jnp.sum(functools.reduce(jnp.add, p), axis=-1, keepdims=True),
                                       (rb, LANES))
            l_ref[rows, :] = alpha * l_ref[rows, :] + row_sum if online else row_sum
            for c, x in zip(lane_tiles, p):
                p_ref[rows, c] = x.astype(BF16)

    m_ref[...] = jnp.full(m_ref.shape, NEG_INF, F32)
    l_ref[...] = jnp.zeros(l_ref.shape, F32)
    acc_ref[...] = jnp.zeros(acc_ref.shape, F32)

    def sel_chunk(c, carry):
        k0 = pl.multiple_of(c * tk, tk)
        picked = jnp.dot(sel, e_ref[c], preferred_element_type=F32)
        kpos = k0 + lax.broadcasted_iota(I32, (1, tk), 1)
        b_ref[:, :tk] = jnp.where((picked > 0.5) & (kpos <= tpos), 0.0, NEG_INF)
        kblk = ks_ref[pl.ds(k0, tk), :]
        for j in range(J):
            softmax_head(j, kblk, tk, online=True)
        pv = jnp.dot(p_ref[:, :tk], vs_ref[pl.ds(k0, tk), :], preferred_element_type=F32)
        acc_ref[...] = a_ref[...] * acc_ref[...] + pv
        return carry

    lax.fori_loop(0, (q0 + tq + tk - 1) // tk, sel_chunk, 0)
    inv_ls = 1.0 / l_ref[...]

    start = pl.multiple_of(jnp.maximum(q0 + tq - span, 0), tq)
    diff = tpos - (start + lax.broadcasted_iota(I32, (1, span), 1))
    b_ref[:, :span] = jnp.where((diff >= 0) & (diff < WINDOW), 0.0, NEG_INF)
    kblk = kw_ref[pl.ds(start, span), :]
    for j in range(J):
        softmax_head(j, kblk, span, online=False)
    o_w = jnp.dot(p_ref[:, :span], vw_ref[pl.ds(start, span), :], preferred_element_type=F32)
    inv_lw = 1.0 / l_ref[...]

    gt = g_ref[...]
    for j in range(J):
        rows = slice(j * tq, (j + 1) * tq)
        o = (gt[:, 3 * j:3 * j + 1] * o_c[rows]
             + (gt[:, 3 * j + 1:3 * j + 2] * inv_ls[rows]) * acc_ref[rows, :]
             + (gt[:, 3 * j + 2:3 * j + 3] * inv_lw[rows]) * o_w[rows])
        o_ref[:, j * hd:(j + 1) * hd] = o.astype(o_ref.dtype)


def _nsa_attn(qr, kcmp, vcmp, ks, vs, kw, vw, gates, batch, seq):
    hd = HEAD_DIM
    J = HEADS_PER_GROUP
    tq = _tile(seq, TQ)
    tk = _tile(seq, TK_SEL)
    n_sel = seq // SEL_BLOCK
    n_top = min(SEL_TOP_N, n_sel)
    span = min(WINDOW + tq, seq)
    width = max(span, tk)
    ncp = kcmp.shape[2]
    nsp = -(-n_sel // 16) * 16
    ci = jnp.arange(ncp, dtype=I32)[None, :] * CMP_STRIDE
    sj = jnp.arange(LANES, dtype=I32)[:, None] * SEL_BLOCK
    ov = jnp.clip(jnp.minimum(ci + CMP_BLOCK, sj + SEL_BLOCK) - jnp.maximum(ci, sj), 0).astype(F32) / CMP_BLOCK
    ov = jnp.where((jnp.arange(ncp)[None, :] < ncp - 1) & (jnp.arange(LANES)[:, None] < n_sel), ov, 0.0)
    kk = jnp.arange(seq, dtype=I32).reshape(seq // tk, 1, tk)
    expand = (kk // SEL_BLOCK == jnp.arange(nsp, dtype=I32)[None, :, None]).astype(BF16)

    r3 = lambda a: a.reshape(batch, seq, a.shape[-1])
    kv_spec = pl.BlockSpec((None, seq, hd), lambda b, g, i: (b, 0, g))
    cmp_spec = pl.BlockSpec((None, None, ncp, hd), lambda b, g, i: (b, g, 0, 0))
    out = pl.pallas_call(
        functools.partial(_nsa_attn_kernel, tq=tq, tk=tk, seq=seq, n_sel=n_sel, n_top=n_top, span=span),
        out_shape=jax.ShapeDtypeStruct((batch, seq, N_HEADS * hd), BF16),
        grid=(batch, N_KV_GROUPS, seq // tq),
        in_specs=[
            pl.BlockSpec((None, tq, J * hd), lambda b, g, i: (b, i, g)),
            cmp_spec, cmp_spec, kv_spec, kv_spec, kv_spec, kv_spec,
            pl.BlockSpec((None, tq, LANES), lambda b, g, i: (b, i, g)),
            pl.BlockSpec(ov.shape, lambda b, g, i: (0, 0)),
            pl.BlockSpec(expand.shape, lambda b, g, i: (0, 0, 0)),
        ],
        out_specs=pl.BlockSpec((None, tq, J * hd), lambda b, g, i: (b, i, g)),
        scratch_shapes=[
            pltpu.VMEM((J * tq, width), BF16),
            pltpu.VMEM((tq, width), F32),
            pltpu.VMEM((J * tq, LANES), F32),
            pltpu.VMEM((J * tq, LANES), F32),
            pltpu.VMEM((J * tq, LANES), F32),
            pltpu.VMEM((J * tq, hd), F32),
        ],
        compiler_params=_params("parallel", "parallel", "parallel"),
        name="nsa_attn",
    )(r3(qr), kcmp, vcmp, r3(ks), r3(vs), r3(kw), r3(vw), r3(gates), ov, expand)
    return out.reshape(batch * seq, N_HEADS * hd)


def _route_kernel(lg_ref, tri_ref, mi_ref, mw_ref, cnt_ref, carry_ref):
    @pl.when(pl.program_id(0) == 0)
    def _():
        carry_ref[...] = jnp.zeros_like(carry_ref)

    lane = lax.broadcasted_iota(I32, lg_ref.shape, 1)
    lg = jnp.where(lane < N_EXPERTS, lg_ref[...], -jnp.inf)
    m1 = jnp.max(lg, axis=-1, keepdims=True)
    i1 = jnp.min(jnp.where(lg == m1, lane, LANES), axis=-1, keepdims=True)
    lg2 = jnp.where(lane == i1, -jnp.inf, lg)
    m2 = jnp.max(lg2, axis=-1, keepdims=True)
    i2 = jnp.min(jnp.where(lg2 == m2, lane, LANES), axis=-1, keepdims=True)
    e2 = jnp.exp(m2 - m1)
    w1 = 1.0 / (1.0 + e2)
    w2 = e2 / (1.0 + e2)
    oh1, oh2 = lane == i1, lane == i2
    onehot = (oh1 | oh2).astype(F32)
    before = jnp.dot(tri_ref[...], onehot.astype(BF16), preferred_element_type=F32) + carry_ref[...]
    pos1 = jnp.sum(jnp.where(oh1, before, 0.0), axis=-1, keepdims=True).astype(I32)
    pos2 = jnp.sum(jnp.where(oh2, before, 0.0), axis=-1, keepdims=True).astype(I32)
    carry_ref[...] += jnp.sum(onehot, axis=0, keepdims=True)
    mi_ref[...] = jnp.where(lane == 0, i1, jnp.where(lane == 1, i2,
                            jnp.where(lane == 2, pos1, jnp.where(lane == 3, pos2, 0))))
    mw_ref[...] = jnp.where(lane == 0, w1, jnp.where(lane == 1, w2, 0.0))
    cnt_ref[...] = carry_ref[...]


def _route(logits):
    t = logits.shape[0]
    tm = _tile(t, 512)
    tri = (jnp.arange(tm)[None, :] < jnp.arange(tm)[:, None]).astype(BF16)
    return pl.pallas_call(
        _route_kernel,
        out_shape=(jax.ShapeDtypeStruct((t, LANES), I32), jax.ShapeDtypeStruct((t, LANES), F32),
                   jax.ShapeDtypeStruct((1, LANES), F32)),
        grid=(t // tm,),
        in_specs=[pl.BlockSpec((tm, LANES), lambda i: (i, 0)), pl.BlockSpec((tm, tm), lambda i: (0, 0))],
        out_specs=(pl.BlockSpec((tm, LANES), lambda i: (i, 0)), pl.BlockSpec((tm, LANES), lambda i: (i, 0)),
                   pl.BlockSpec((1, LANES), lambda i: (0, 0))),
        scratch_shapes=[pltpu.VMEM((1, LANES), F32)],
        compiler_params=_params("arbitrary"),
        name="moe_route",
    )(logits, tri)


def _row_copy(src_hbm, row, buf, slot, r, sem):
    return pltpu.make_async_copy(src_hbm.at[pl.ds(row, 1), :], buf.at[slot, pl.ds(r, 1), :], sem.at[slot])


def _issue_rows(idx_ref, src_hbm, buf, slot, base, sem, rows):
    def body(r, carry):
        _row_copy(src_hbm, idx_ref[0, r], buf, slot, base + r, sem).start()
        return carry

    lax.fori_loop(0, rows, body, 0, unroll=8)


def _wait_rows(src_hbm, buf, slot, sem, rows):
    def body(r, carry):
        _row_copy(src_hbm, 0, buf, slot, r, sem).wait()
        return carry

    lax.fori_loop(0, rows, body, 0, unroll=8)


def _dispatch_kernel(idx_ref, nxt_ref, src_hbm, o_ref, buf, sem, *, rows):
    i = pl.program_id(0)
    slot = i % 2

    @pl.when(i == 0)
    def _():
        _issue_rows(idx_ref, src_hbm, buf, 0, 0, sem, rows)

    @pl.when(i + 1 < pl.num_programs(0))
    def _():
        _issue_rows(nxt_ref, src_hbm, buf, 1 - slot, 0, sem, rows)

    _wait_rows(src_hbm, buf, slot, sem, rows)
    o_ref[...] = buf[slot].astype(o_ref.dtype)


def _dispatch_rows(src, idx):
    n = idx.shape[0]
    d = src.shape[1]
    tg = _tile(n, TG)
    steps = n // tg
    smem = lambda f: pl.BlockSpec((None, 1, tg), f, memory_space=pltpu.SMEM)
    idx3 = idx.reshape(steps, 1, tg)
    return pl.pallas_call(
        functools.partial(_dispatch_kernel, rows=tg),
        out_shape=jax.ShapeDtypeStruct((n, d), BF16),
        grid=(steps,),
        in_specs=[
            smem(lambda i: (i, 0, 0)),
            smem(lambda i: (jnp.minimum(i + 1, steps - 1), 0, 0)),
            pl.BlockSpec(memory_space=pl.ANY),
        ],
        out_specs=pl.BlockSpec((tg, d), lambda i: (i, 0)),
        scratch_shapes=[pltpu.VMEM((2, tg, d), F32), pltpu.SemaphoreType.DMA((2,))],
        compiler_params=_params("arbitrary"),
        name="moe_dispatch",
    )(idx3, idx3, src)


def _combine_kernel(a_ref, b_ref, na_ref, nb_ref, y_hbm, x_ref, g_ref, mw_ref, o_ref, buf, sem, *, rows):
    i = pl.program_id(0)
    slot = i % 2

    def issue(first_ref, second_ref, s):
        _issue_rows(first_ref, y_hbm, buf, s, 0, sem, rows)
        _issue_rows(second_ref, y_hbm, buf, s, rows, sem, rows)

    @pl.when(i == 0)
    def _():
        issue(a_ref, b_ref, 0)

    @pl.when(i + 1 < pl.num_programs(0))
    def _():
        issue(na_ref, nb_ref, 1 - slot)

    _wait_rows(y_hbm, buf, slot, sem, 2 * rows)
    mw = mw_ref[...]
    f = mw[:, 0:1] * buf[slot, 0:rows, :] + mw[:, 1:2] * buf[slot, rows:2 * rows, :]
    o_ref[...] = x_ref[...] + g_ref[...] * f


def _combine_rows(ys, d1, d2, meta_w, x, gate, seq):
    t, d = x.shape
    b = gate.shape[0]
    tc = _tile(seq, TCMB)
    steps = t // tc
    smem = lambda f: pl.BlockSpec((None, 1, tc), f, memory_space=pltpu.SMEM)
    cur, nxt = (lambda i: (i, 0, 0)), (lambda i: (jnp.minimum(i + 1, steps - 1), 0, 0))
    a3, b3 = d1.reshape(steps, 1, tc), d2.reshape(steps, 1, tc)
    return pl.pallas_call(
        functools.partial(_combine_kernel, rows=tc),
        out_shape=jax.ShapeDtypeStruct((t, d), F32),
        grid=(steps,),
        in_specs=[
            smem(cur), smem(cur), smem(nxt), smem(nxt),
            pl.BlockSpec(memory_space=pl.ANY),
            pl.BlockSpec((tc, d), lambda i: (i, 0)),
            pl.BlockSpec((None, 1, d), lambda i: (i * tc // seq, 0, 0)),
            pl.BlockSpec((tc, LANES), lambda i: (i, 0)),
        ],
        out_specs=pl.BlockSpec((tc, d), lambda i: (i, 0)),
        scratch_shapes=[pltpu.VMEM((2, 2 * tc, d), F32), pltpu.SemaphoreType.DMA((2,))],
        compiler_params=_params("arbitrary"),
        name="moe_combine",
    )(a3, b3, a3, b3, ys, x, gate.reshape(b, 1, d), meta_w)


def _moe_ffn_residual(x, hf, logits, gate, w_gate, w_up, w_down, seq):
    t, d = hf.shape
    n_exp = w_gate.shape[0]
    tm = _tile(t, TM_MOE)
    meta_i, meta_w, counts = _route(logits)
    e1, e2, p1, p2 = meta_i[:, 0], meta_i[:, 1], meta_i[:, 2], meta_i[:, 3]
    cnt = counts[0, :n_exp].astype(I32)
    padded = (cnt + tm - 1) // tm * tm
    ends = jnp.cumsum(padded)
    starts = ends - padded
    d1, d2 = starts[e1] + p1, starts[e2] + p2
    n_rows = 2 * t + n_exp * tm
    tok = jnp.arange(t, dtype=I32)
    src = jnp.zeros((n_rows,), I32).at[d1].set(tok).at[d2].set(tok)
    tiles = jnp.arange(n_rows // tm, dtype=I32)
    tile_expert = jnp.minimum(jnp.sum((tiles * tm)[:, None] >= ends[None, :], axis=1), n_exp - 1).astype(I32)
    tile_row = jnp.minimum(tiles, ends[-1] // tm - 1)
    tile_first = jnp.concatenate([jnp.ones((1,), I32), (tile_expert[1:] != tile_expert[:-1]).astype(I32)])
    sched = (tile_expert, tile_row, tile_first)

    xs = _dispatch_rows(hf, src)
    hs = _ws_mm(_ws_glu_kernel, xs, (w_gate, w_up), sched, tm, TN_MOE, BF16, "moe_glu")
    ys = _ws_mm(_ws_mm_kernel, hs, (w_down,), sched, tm, TN_MOE_DOWN, F32, "moe_down")
    return _combine_rows(ys, d1, d2, meta_w, x, gate, seq)


def kernel(x, c, ada_w, ada_b, norm_mix_g, norm_ffn_g, conv_w_in, conv_w, conv_w_out, nsa_w_in, nsa_w_o,
           nsa_q_norm_g, nsa_k_norm_g, cmp_pos, cmp_w1, cmp_w2, ffn_w_gate, ffn_w_up, ffn_w_down,
           router_w, router_b, moe_w_gate, moe_w_up, moe_w_down):
    batch, seq, d = x.shape
    t = batch * seq
    bf = lambda a: a.astype(BF16)

    mod = _ada_mod(c, ada_w, ada_b)
    sh_m, sc_m, g_m, sh_f, sc_f, g_f = [mod[:, :, i * d:(i + 1) * d] for i in range(6)]
    xt = x.reshape(t, d)

    hm = _norm_mod(xt, norm_mix_g[0], sc_m[0], sh_m[0], seq)
    z = _conv_mix(hm, bf(conv_w_in[0]), conv_w[0], batch, seq)
    xt = _mm_res(z, bf(conv_w_out[0]), xt, g_m[0], seq, "conv_out")
    hf = _norm_mod(xt, norm_ffn_g[0], sc_f[0], sh_f[0], seq)
    tm_ffn = _tile(t, 2 * TM)
    hid = _ws_mm(_ws_glu_kernel, hf, (ffn_w_gate[0:1], ffn_w_up[0:1]), _dense_sched(t // tm_ffn), tm_ffn,
                 TN_GLU, BF16, "ffn_glu")
    xt = _mm_res(hid, bf(ffn_w_down[0]), xt, g_f[0], seq, "ffn_down", tm=TM_DOWN, tn=TN_DOWN)

    hm = _norm_mod(xt, norm_mix_g[1], sc_m[1], sh_m[1], seq)
    qkv_w = N_HEADS * HEAD_DIM + 6 * KV_WIDTH
    w_in = nsa_w_in[0]
    proj = _mm(hm, bf(w_in[:, :qkv_w]), F32, "nsa_in")
    w_gates = jnp.pad(w_in[:, qkv_w:], ((0, 0), (0, LANES - 3 * N_HEADS)))
    gates_pre = _mm(hm, bf(w_gates), F32, "nsa_gates")
    pos = jnp.arange(seq, dtype=I32)
    cos, sin = _rope_tables(pos)
    qr, ks, vs, kw, vw, gates = _nsa_prep(proj, gates_pre, cos, sin, nsa_q_norm_g[0], nsa_k_norm_g[0], seq)
    cmp_end = jnp.arange(seq // CMP_STRIDE, dtype=I32) * CMP_STRIDE + CMP_BLOCK - 1
    cos_c, sin_c = _rope_tables(cmp_end)
    kcmp, vcmp = _compress(proj.reshape(batch, seq, qkv_w), bf(cmp_w1[0]), bf(cmp_w2[0]), cmp_pos[0],
                           nsa_k_norm_g[0, 0:1], cos_c, sin_c)
    attn = _nsa_attn(qr, kcmp, vcmp, ks, vs, kw, vw, gates, batch, seq)
    xt = _mm_res(attn, bf(nsa_w_o[0]), xt, g_m[1], seq, "nsa_out")

    n_exp = router_w.shape[2]
    rw = jnp.pad(router_w[0], ((0, 0), (0, LANES - n_exp)))
    rb = jnp.pad(router_b[0], (0, LANES - n_exp)).reshape(1, LANES)
    hf, logits = _norm_mod(xt, norm_ffn_g[1], sc_f[1], sh_f[1], seq, router=(rw, rb))
    out = _moe_ffn_residual(xt, hf, logits, g_f[1], moe_w_gate[0], moe_w_up[0], moe_w_down[0], seq)
    return out.reshape(batch, seq, d)
```

```python
import functools

import jax
import jax.numpy as jnp
from jax import lax
from jax.experimental import pallas as pl
from jax.experimental.pallas import tpu as pltpu

F32 = jnp.float32
BF16 = jnp.bfloat16
I32 = jnp.int32

EPS = 1e-6
NEG_INF = -1e30
FORCE_BONUS = 1e4
ROPE_THETA = 10000.0
LOG2E = 1.4426950408889634

CONV_WIDTH = 3
N_HEADS = 32
HEAD_DIM = 128
N_KV_GROUPS = 4
HEADS_PER_GROUP = N_HEADS // N_KV_GROUPS
KV_WIDTH = N_KV_GROUPS * HEAD_DIM
CMP_BLOCK = 32
CMP_STRIDE = 16
SEL_BLOCK = 64
SEL_TOP_N = 16
WINDOW = 512
N_EXPERTS = 8

LANES = 128
VMEM_LIMIT_V7X = 60 * 1024 * 1024

TM = 1024
TN = 1024
TN_RES = 1024
TN_GLU = 256
TM_DOWN = 512
TN_DOWN = 512
TM_ROW = 512
TC_CONV = 512
TQ = 256
TK_SEL = 512
ATTN_ROW_BLOCK = 64
TM_MOE = 512
TN_MOE = 512
TN_MOE_DOWN = 1024
TG = 512
TCMB = 256


def _params(*sem):
    return pltpu.CompilerParams(dimension_semantics=sem, vmem_limit_bytes=VMEM_LIMIT_V7X)


def _tile(full, want):
    return want if full % want == 0 else full


def _ada_kernel(c_ref, w_ref, b_ref, o_ref):
    c = c_ref[...]
    cond = c * jax.nn.sigmoid(c)
    acc = jnp.dot(cond.astype(BF16), w_ref[...].astype(BF16), preferred_element_type=F32)
    o_ref[...] = acc + b_ref[...]


def _ada_mod(c, ada_w, ada_b):
    depth, d, n = ada_w.shape
    b = c.shape[0]
    tn = _tile(n, 512)
    return pl.pallas_call(
        _ada_kernel,
        out_shape=jax.ShapeDtypeStruct((depth, b, n), F32),
        grid=(depth, n // tn),
        in_specs=[
            pl.BlockSpec((b, d), lambda l, j: (0, 0)),
            pl.BlockSpec((None, d, tn), lambda l, j: (l, 0, j)),
            pl.BlockSpec((None, 1, tn), lambda l, j: (l, 0, j)),
        ],
        out_specs=pl.BlockSpec((None, b, tn), lambda l, j: (l, 0, j)),
        compiler_params=_params("parallel", "parallel"),
        name="ada_mod",
    )(c, ada_w, ada_b.reshape(depth, 1, n))


def _norm_mod_value(x_ref, g_ref, sc_ref, sh_ref):
    x = x_ref[...]
    ms = jnp.mean(x * x, axis=-1, keepdims=True)
    y = x * lax.rsqrt(ms + EPS) * g_ref[...]
    return y * (1.0 + sc_ref[...]) + sh_ref[...]


def _norm_mod_kernel(x_ref, g_ref, sc_ref, sh_ref, o_ref):
    o_ref[...] = _norm_mod_value(x_ref, g_ref, sc_ref, sh_ref).astype(o_ref.dtype)


def _norm_mod_router_kernel(x_ref, g_ref, sc_ref, sh_ref, rw_ref, rb_ref, o_ref, lg_ref):
    h = _norm_mod_value(x_ref, g_ref, sc_ref, sh_ref)
    o_ref[...] = h.astype(o_ref.dtype)
    lg_ref[...] = jnp.dot(h, rw_ref[...], preferred_element_type=F32,
                          precision=lax.Precision.HIGHEST) + rb_ref[...]


def _norm_mod(x, g, sc, sh, seq, router=None):
    t, d = x.shape
    b = sc.shape[0]
    tm = _tile(seq, TM_ROW)
    row_specs = [
        pl.BlockSpec((tm, d), lambda i: (i, 0)),
        pl.BlockSpec((1, d), lambda i: (0, 0)),
        pl.BlockSpec((None, 1, d), lambda i: (i * tm // seq, 0, 0)),
        pl.BlockSpec((None, 1, d), lambda i: (i * tm // seq, 0, 0)),
    ]
    args = [x, g.reshape(1, d), sc.reshape(b, 1, d), sh.reshape(b, 1, d)]
    if router is None:
        return pl.pallas_call(
            _norm_mod_kernel,
            out_shape=jax.ShapeDtypeStruct((t, d), BF16),
            grid=(t // tm,),
            in_specs=row_specs,
            out_specs=pl.BlockSpec((tm, d), lambda i: (i, 0)),
            compiler_params=_params("parallel"),
            name="norm_mod",
        )(*args)
    rw, rb = router
    return pl.pallas_call(
        _norm_mod_router_kernel,
        out_shape=(jax.ShapeDtypeStruct((t, d), F32), jax.ShapeDtypeStruct((t, LANES), F32)),
        grid=(t // tm,),
        in_specs=row_specs + [
            pl.BlockSpec((d, LANES), lambda i: (0, 0)),
            pl.BlockSpec((1, LANES), lambda i: (0, 0)),
        ],
        out_specs=(pl.BlockSpec((tm, d), lambda i: (i, 0)), pl.BlockSpec((tm, LANES), lambda i: (i, 0))),
        compiler_params=_params("parallel"),
        name="norm_mod_router",
    )(*args, rw, rb)


def _mm_kernel(x_ref, w_ref, o_ref):
    o_ref[...] = jnp.dot(x_ref[...], w_ref[...], preferred_element_type=F32).astype(o_ref.dtype)


def _mm(x, w, out_dtype, name, tm=TM, tn=TN, n=None):
    m, k = x.shape
    n = w.shape[1] if n is None else n
    tm, tn = _tile(m, tm), _tile(n, tn)
    return pl.pallas_call(
        _mm_kernel,
        out_shape=jax.ShapeDtypeStruct((m, n), out_dtype),
        grid=(m // tm, n // tn),
        in_specs=[pl.BlockSpec((tm, k), lambda i, j: (i, 0)), pl.BlockSpec((k, tn), lambda i, j: (0, j))],
        out_specs=pl.BlockSpec((tm, tn), lambda i, j: (i, j)),
        compiler_params=_params("parallel", "parallel"),
        name=name,
    )(x, w)


def _mm_res_kernel(x_ref, w_ref, res_ref, g_ref, o_ref):
    o_ref[...] = res_ref[...] + g_ref[...] * jnp.dot(x_ref[...], w_ref[...], preferred_element_type=F32)


def _mm_res(x, w, res, gate, seq, name, tm=TM, tn=TN_RES):
    m, k = x.shape
    n = w.shape[1]
    b = gate.shape[0]
    tm, tn = _tile(seq, tm), _tile(n, tn)
    return pl.pallas_call(
        _mm_res_kernel,
        out_shape=jax.ShapeDtypeStruct((m, n), F32),
        grid=(m // tm, n // tn),
        in_specs=[
            pl.BlockSpec((tm, k), lambda i, j: (i, 0)),
            pl.BlockSpec((k, tn), lambda i, j: (0, j)),
            pl.BlockSpec((tm, tn), lambda i, j: (i, j)),
            pl.BlockSpec((None, 1, tn), lambda i, j: (i * tm // seq, 0, j)),
        ],
        out_specs=pl.BlockSpec((tm, tn), lambda i, j: (i, j)),
        compiler_params=_params("parallel", "parallel"),
        name=name,
    )(x, w, res, gate.reshape(b, 1, n))


def _silu_mul(a, b):
    return (a * jax.nn.sigmoid(a)) * b


def _ws_glu_kernel(te_ref, tr_ref, tf_ref, x_ref, wg_ref, wu_ref, o_ref, wg_bf, wu_bf):
    i = pl.program_id(1)

    @pl.when(tf_ref[i] == 1)
    def _():
        wg_bf[...] = wg_ref[...].astype(BF16)
        wu_bf[...] = wu_ref[...].astype(BF16)

    @pl.when(tr_ref[i] == i)
    def _():
        x = x_ref[...]
        a = jnp.dot(x, wg_bf[...], preferred_element_type=F32)
        b = jnp.dot(x, wu_bf[...], preferred_element_type=F32)
        o_ref[...] = _silu_mul(a, b).astype(o_ref.dtype)

    @pl.when(tr_ref[i] != i)
    def _():
        o_ref[...] = jnp.zeros_like(o_ref)


def _ws_mm_kernel(te_ref, tr_ref, tf_ref, x_ref, w_ref, o_ref, w_bf):
    i = pl.program_id(1)

    @pl.when(tf_ref[i] == 1)
    def _():
        w_bf[...] = w_ref[...].astype(BF16)

    @pl.when(tr_ref[i] == i)
    def _():
        o_ref[...] = jnp.dot(x_ref[...], w_bf[...], preferred_element_type=F32).astype(o_ref.dtype)

    @pl.when(tr_ref[i] != i)
    def _():
        o_ref[...] = jnp.zeros_like(o_ref)


def _ws_mm(kernel, xs, ws, sched, tm, tn, out_dtype, name):
    r, k = xs.shape
    n = ws[0].shape[2]
    tn = _tile(n, tn)
    w_spec = pl.BlockSpec((None, k, tn), lambda j, i, te, tr, tf: (te[i], 0, j))
    return pl.pallas_call(
        kernel,
        out_shape=jax.ShapeDtypeStruct((r, n), out_dtype),
        grid_spec=pltpu.PrefetchScalarGridSpec(
            num_scalar_prefetch=3,
            grid=(n // tn, r // tm),
            in_specs=[pl.BlockSpec((tm, k), lambda j, i, te, tr, tf: (tr[i], 0))] + [w_spec] * len(ws),
            out_specs=pl.BlockSpec((tm, tn), lambda j, i, te, tr, tf: (i, j)),
            scratch_shapes=[pltpu.VMEM((k, tn), BF16)] * len(ws),
        ),
        compiler_params=_params("parallel", "arbitrary"),
        name=name,
    )(*sched, xs, *ws)


def _dense_sched(n_tiles):
    tiles = jnp.arange(n_tiles, dtype=I32)
    return jnp.zeros((n_tiles,), I32), tiles, (tiles == 0).astype(I32)


SUBLANES = 8


def _conv_mix_kernel(x_ref, wb_ref, wc_ref, wu_ref, cw_ref, o_ref, tail_ref):
    @pl.when(pl.program_id(2) == 0)
    def _():
        tail_ref[...] = jnp.zeros_like(tail_ref)

    x = x_ref[...]
    v = (jnp.dot(x, wc_ref[...], preferred_element_type=F32)
         * jnp.dot(x, wu_ref[...], preferred_element_type=F32))
    prev = tail_ref[...]
    tail_ref[...] = v[v.shape[0] - SUBLANES:, :]
    rows = lax.broadcasted_iota(I32, v.shape, 0)
    rest = v[SUBLANES:, :]
    v1 = jnp.where(rows >= 1, pltpu.roll(v, 1, 0), jnp.concatenate([pltpu.roll(prev, 1, 0), rest], axis=0))
    v2 = jnp.where(rows >= 2, pltpu.roll(v, 2, 0), jnp.concatenate([pltpu.roll(prev, 2, 0), rest], axis=0))
    w = cw_ref[...]
    y = w[0:1, :] * v2 + w[1:2, :] * v1 + w[2:3, :] * v
    o_ref[...] = (jnp.dot(x, wb_ref[...], preferred_element_type=F32) * y).astype(o_ref.dtype)


def _conv_mix(h, w_in, conv_w, batch, seq):
    t, k = h.shape
    d = conv_w.shape[1]
    tm = _tile(seq, TM)
    tc = _tile(d, TC_CONV)
    nc, nh = d // tc, seq // tm
    w_spec = lambda part: pl.BlockSpec((k, tc), lambda b, j, r: (0, part * nc + j))
    return pl.pallas_call(
        _conv_mix_kernel,
        out_shape=jax.ShapeDtypeStruct((t, d), BF16),
        grid=(batch, nc, nh),
        in_specs=[
            pl.BlockSpec((tm, k), lambda b, j, r: (b * nh + r, 0)),
            w_spec(0), w_spec(1), w_spec(2),
            pl.BlockSpec((CONV_WIDTH, tc), lambda b, j, r: (0, j)),
        ],
        out_specs=pl.BlockSpec((tm, tc), lambda b, j, r: (b * nh + r, j)),
        scratch_shapes=[pltpu.VMEM((SUBLANES, tc), F32)],
        compiler_params=_params("parallel", "parallel", "arbitrary"),
        name="conv_mix",
    )(h, w_in, w_in, w_in, conv_w)


def _rope_tables(pos):
    half = HEAD_DIM // 2
    inv = ROPE_THETA ** (-jnp.arange(half, dtype=F32) / half)
    ang = pos.astype(F32)[:, None] * inv[None, :]
    cos, sin = jnp.cos(ang), jnp.sin(ang)
    return jnp.concatenate([cos, cos], axis=-1), jnp.concatenate([-sin, sin], axis=-1)


def _norm_rope(x, g, cos, sin_signed, ones=None):
    if ones is None:
        ms = jnp.mean(x * x, axis=-1, keepdims=True)
    else:
        ms = jnp.dot((x * x).astype(BF16), ones, preferred_element_type=F32) * (1.0 / HEAD_DIM)
    xn = x * lax.rsqrt(ms + EPS) * g
    return xn * cos + pltpu.roll(xn, HEAD_DIM // 2, 1) * sin_signed


def _nsa_prep_kernel(p_ref, gp_ref, cos_ref, sin_ref, qg_ref, kg_ref,
                     q_o, ks_o, vs_o, kw_o, vw_o, g_o, *, scale):
    cos, sin = cos_ref[...], sin_ref[...]
    hd = HEAD_DIM
    qg = qg_ref[...]
    ones = jnp.ones((hd, hd), BF16)
    for h in range(N_HEADS):
        x = p_ref[:, h * hd:(h + 1) * hd]
        q_o[:, h * hd:(h + 1) * hd] = (_norm_rope(x, qg, cos, sin, ones) * scale).astype(q_o.dtype)
    kv0 = N_HEADS * hd
    for g in range(N_KV_GROUPS):
        def col(i, g=g):
            c0 = kv0 + i * KV_WIDTH + g * hd
            return p_ref[:, c0:c0 + hd]
        sl = slice(g * hd, (g + 1) * hd)
        ks_o[:, sl] = _norm_rope(col(2), kg_ref[1:2, :], cos, sin, ones).astype(ks_o.dtype)
        vs_o[:, sl] = col(3).astype(vs_o.dtype)
        kw_o[:, sl] = _norm_rope(col(4), kg_ref[2:3, :], cos, sin, ones).astype(kw_o.dtype)
        vw_o[:, sl] = col(5).astype(vw_o.dtype)
    sig = jax.nn.sigmoid(gp_ref[...])
    per_group = 3 * HEADS_PER_GROUP
    for g in range(N_KV_GROUPS):
        shift = (LANES - g * per_group) % LANES
        g_o[:, g * LANES:(g + 1) * LANES] = sig if shift == 0 else pltpu.roll(sig, shift, 1)


def _nsa_prep(proj, gates_pre, cos, sin, q_g, k_g, seq):
    t = proj.shape[0]
    tm = _tile(seq, 256)
    nseq = seq // tm
    qd = N_HEADS * HEAD_DIM
    kvspec = pl.BlockSpec((tm, KV_WIDTH), lambda i: (i, 0))
    kvshape = jax.ShapeDtypeStruct((t, KV_WIDTH), BF16)
    return pl.pallas_call(
        functools.partial(_nsa_prep_kernel, scale=HEAD_DIM ** -0.5 * LOG2E),
        out_shape=(jax.ShapeDtypeStruct((t, qd), BF16), kvshape, kvshape, kvshape, kvshape,
                   jax.ShapeDtypeStruct((t, N_KV_GROUPS * LANES), F32)),
        grid=(t // tm,),
        in_specs=[
            pl.BlockSpec((tm, proj.shape[1]), lambda i: (i, 0)),
            pl.BlockSpec((tm, LANES), lambda i: (i, 0)),
            pl.BlockSpec((tm, HEAD_DIM), lambda i: (i % nseq, 0)),
            pl.BlockSpec((tm, HEAD_DIM), lambda i: (i % nseq, 0)),
            pl.BlockSpec((1, HEAD_DIM), lambda i: (0, 0)),
            pl.BlockSpec((3, HEAD_DIM), lambda i: (0, 0)),
        ],
        out_specs=(pl.BlockSpec((tm, qd), lambda i: (i, 0)), kvspec, kvspec, kvspec, kvspec,
                   pl.BlockSpec((tm, N_KV_GROUPS * LANES), lambda i: (i, 0))),
        compiler_params=_params("parallel"),
        name="nsa_prep",
    )(proj, gates_pre, cos, sin, q_g.reshape(1, HEAD_DIM), k_g)


def _compress_kernel(kc_ref, vc_ref, w1_ref, w2_ref, pos_ref, kg_ref, cos_ref, sin_ref,
                     ko_ref, vo_ref, *, nchunk):
    def comp(x_ref, i):
        first = jnp.zeros((nchunk, w1_ref.shape[-1]), F32)
        second = jnp.zeros_like(first)
        for l in range(CMP_STRIDE):
            xl = x_ref[pl.ds(l, nchunk, stride=CMP_STRIDE), :]
            xa = (xl + pos_ref[i, l:l + 1, :]).astype(BF16)
            xb = (xl + pos_ref[i, CMP_STRIDE + l:CMP_STRIDE + l + 1, :]).astype(BF16)
            first += jnp.dot(xa, w1_ref[i, l], preferred_element_type=F32)
            second += jnp.dot(xb, w1_ref[i, CMP_STRIDE + l], preferred_element_type=F32)
        pre = first + pltpu.roll(second, nchunk - 1, 0)
        hid = pre * jax.nn.sigmoid(pre)
        return jnp.dot(hid.astype(BF16), w2_ref[i], preferred_element_type=F32)

    k = comp(kc_ref, 0)
    ko_ref[...] = _norm_rope(k, kg_ref[...], cos_ref[...], sin_ref[...]).astype(ko_ref.dtype)
    vo_ref[...] = comp(vc_ref, 1).astype(vo_ref.dtype)


def _compress(proj3, w1, w2, cmp_pos, kg0, cos_c, sin_c):
    batch, seq, _ = proj3.shape
    nchunk = seq // CMP_STRIDE
    hd = HEAD_DIM
    cb = N_HEADS
    out = jax.ShapeDtypeStruct((batch, N_KV_GROUPS, nchunk, hd), BF16)
    ospec = pl.BlockSpec((None, None, nchunk, hd), lambda b, g: (b, g, 0, 0))
    full = lambda a: pl.BlockSpec(a.shape, lambda b, g: (0,) * a.ndim)
    return pl.pallas_call(
        functools.partial(_compress_kernel, nchunk=nchunk),
        out_shape=(out, out),
        grid=(batch, N_KV_GROUPS),
        in_specs=[
            pl.BlockSpec((None, seq, hd), lambda b, g: (b, 0, cb + g)),
            pl.BlockSpec((None, seq, hd), lambda b, g: (b, 0, cb + N_KV_GROUPS + g)),
            full(w1), full(w2), full(cmp_pos), full(kg0), full(cos_c), full(sin_c),
        ],
        out_specs=(ospec, ospec),
        compiler_params=_params("parallel", "parallel"),
        name="nsa_compress",
    )(proj3, proj3, w1, w2, cmp_pos, kg0, cos_c, sin_c)


def _nsa_attn_kernel(q_ref, kc_ref, vc_ref, ks_ref, vs_ref, kw_ref, vw_ref, g_ref, ov_ref, e_ref,
                     o_ref, p_ref, b_ref, m_ref, l_ref, a_ref, acc_ref,
                     *, tq, tk, seq, n_sel, n_top, span):
    J = HEADS_PER_GROUP
    hd = HEAD_DIM
    q0 = pl.program_id(2) * tq
    q = q_ref[...]
    qa = jnp.concatenate([q[:, j * hd:(j + 1) * hd] for j in range(J)], axis=0)
    tpos = q0 + lax.broadcasted_iota(I32, (tq, 1), 0)

    def scores(kblk):
        return lax.dot_general(qa, kblk, (((1,), (1,)), ((), ())), preferred_element_type=F32)

    ncp = kc_ref.shape[0]
    n_idx = lax.broadcasted_iota(I32, (1, ncp), 1)
    mask_c = (n_idx * CMP_STRIDE + (CMP_BLOCK - 1)) <= tpos
    s3 = jnp.where(mask_c[None], scores(kc_ref[...]).reshape(J, tq, ncp), NEG_INF)
    m = jnp.max(s3, axis=-1, keepdims=True)
    p = jnp.where(mask_c[None], jnp.exp2(s3 - m), 0.0)
    l = jnp.sum(p, axis=-1, keepdims=True)
    pc = p * jnp.where(l > 0.0, 1.0 / l, 0.0)
    o_c = jnp.dot(pc.reshape(J * tq, ncp).astype(BF16), vc_ref[...], preferred_element_type=F32)

    nsp = e_ref.shape[1]
    imp = lax.dot_general(ov_ref[...], jnp.sum(pc, axis=0), (((1,), (1,)), ((), ())),
                          preferred_element_type=F32, precision=lax.Precision.HIGHEST)[:nsp]
    blk = lax.broadcasted_iota(I32, (nsp, 1), 0)
    tpos_l = q0 + lax.broadcasted_iota(I32, (1, tq), 1)
    cur = tpos_l // SEL_BLOCK
    forced = (blk == 0) | (blk == cur) | (blk == cur - 1)
    imp = jnp.where(blk * SEL_BLOCK <= tpos_l, imp + FORCE_BONUS * forced.astype(F32), NEG_INF)
    rank = jnp.zeros((nsp, tq), F32)
    for mp in range(n_sel):
        row = imp[mp:mp + 1, :]
        rank += jnp.where(blk > mp, (row >= imp).astype(F32), (row > imp).astype(F32))
    sel = jnp.transpose(((rank < n_top) & (blk < n_sel)).astype(F32)).astype(BF16)

    rb = ATTN_ROW_BLOCK

    def softmax_head(j, kblk, width, first, keep_max):
        s_head = lax.dot_general(q[:, j * hd:(j + 1) * hd], kblk, (((1,), (1,)), ((), ())),
                                 preferred_element_type=F32)
        lane_tiles = [slice(c, c + LANES) for c in range(0, width, LANES)]
        for h0 in range(0, tq, rb):
            rows = slice(j * tq + h0, j * tq + h0 + rb)
            s = [s_head[h0:h0 + rb, c] + b_ref[h0:h0 + rb, c] for c in lane_tiles]
            m_new = jnp.broadcast_to(jnp.max(functools.reduce(jnp.maximum, s), axis=-1, keepdims=True),
                                     (rb, LANES))
            if not first:
                m_old = m_ref[rows, :]
                m_new = jnp.maximum(m_old, m_new)
                alpha = jnp.exp2(m_old - m_new)
                a_ref[rows, :] = alpha
            if keep_max:
                m_ref[rows, :] = m_new
            p = [jnp.exp2(x - m_new) for x in s]
            row_sum = jnp.broadcast_to(jnp.sum(functools.reduce(jnp.add, p), axis=-1, keepdims=True),
                                       (rb, LANES))
            l_ref[rows, :] = row_sum if first else alpha * l_ref[rows, :] + row_sum
            for c, x in zip(lane_tiles, p):
                p_ref[rows, c] = x.astype(BF16)

    def sel_chunk(c, first):
        k0 = c * tk if first else pl.multiple_of(c * tk, tk)
        picked = jnp.dot(sel, e_ref[c], preferred_element_type=F32)
        kpos = k0 + lax.broadcasted_iota(I32, (1, tk), 1)
        b_ref[:, :tk] = jnp.where((picked > 0.5) & (kpos <= tpos), 0.0, NEG_INF)
        kblk = ks_ref[pl.ds(k0, tk), :]
        for j in range(J):
            softmax_head(j, kblk, tk, first, keep_max=True)
        pv = jnp.dot(p_ref[:, :tk], vs_ref[pl.ds(k0, tk), :], preferred_element_type=F32)
        acc_ref[...] = pv if first else a_ref[...] * acc_ref[...] + pv

    sel_chunk(0, True)

    def later_chunk(c, carry):
        sel_chunk(c, False)
        return carry

    lax.fori_loop(1, (q0 + tq + tk - 1) // tk, later_chunk, 0)
    inv_ls = 1.0 / l_ref[...]

    start = pl.multiple_of(jnp.maximum(q0 + tq - span, 0), tq)
    diff = tpos - (start + lax.broadcasted_iota(I32, (1, span), 1))
    b_ref[:, :span] = jnp.where((diff >= 0) & (diff < WINDOW), 0.0, NEG_INF)
    kblk = kw_ref[pl.ds(start, span), :]
    for j in range(J):
        softmax_head(j, kblk, span, True, keep_max=False)
    o_w = jnp.dot(p_ref[:, :span], vw_ref[pl.ds(start, span), :], preferred_element_type=F32)
    inv_lw = 1.0 / l_ref[...]

    gt = g_ref[...]
    for j in range(J):
        rows = slice(j * tq, (j + 1) * tq)
        o = (gt[:, 3 * j:3 * j + 1] * o_c[rows]
             + (gt[:, 3 * j + 1:3 * j + 2] * inv_ls[rows]) * acc_ref[rows, :]
             + (gt[:, 3 * j + 2:3 * j + 3] * inv_lw[rows]) * o_w[rows])
        o_ref[:, j * hd:(j + 1) * hd] = o.astype(o_ref.dtype)


def _nsa_attn(qr, kcmp, vcmp, ks, vs, kw, vw, gates, batch, seq):
    hd = HEAD_DIM
    J = HEADS_PER_GROUP
    tq = _tile(seq, TQ)
    tk = _tile(seq, TK_SEL)
    n_sel = seq // SEL_BLOCK
    n_top = min(SEL_TOP_N, n_sel)
    span = min(WINDOW + tq, seq)
    width = max(span, tk)
    ncp = kcmp.shape[2]
    nsp = -(-n_sel // 16) * 16
    ci = jnp.arange(ncp, dtype=I32)[None, :] * CMP_STRIDE
    sj = jnp.arange(LANES, dtype=I32)[:, None] * SEL_BLOCK
    ov = jnp.clip(jnp.minimum(ci + CMP_BLOCK, sj + SEL_BLOCK) - jnp.maximum(ci, sj), 0).astype(F32) / CMP_BLOCK
    ov = jnp.where((jnp.arange(ncp)[None, :] < ncp - 1) & (jnp.arange(LANES)[:, None] < n_sel), ov, 0.0)
    kk = jnp.arange(seq, dtype=I32).reshape(seq // tk, 1, tk)
    expand = (kk // SEL_BLOCK == jnp.arange(nsp, dtype=I32)[None, :, None]).astype(BF16)

    r3 = lambda a: a.reshape(batch, seq, a.shape[-1])
    kv_spec = pl.BlockSpec((None, seq, hd), lambda b, g, i: (b, 0, g))
    cmp_spec = pl.BlockSpec((None, None, ncp, hd), lambda b, g, i: (b, g, 0, 0))
    out = pl.pallas_call(
        functools.partial(_nsa_attn_kernel, tq=tq, tk=tk, seq=seq, n_sel=n_sel, n_top=n_top, span=span),
        out_shape=jax.ShapeDtypeStruct((batch, seq, N_HEADS * hd), BF16),
        grid=(batch, N_KV_GROUPS, seq // tq),
        in_specs=[
            pl.BlockSpec((None, tq, J * hd), lambda b, g, i: (b, i, g)),
            cmp_spec, cmp_spec, kv_spec, kv_spec, kv_spec, kv_spec,
            pl.BlockSpec((None, tq, LANES), lambda b, g, i: (b, i, g)),
            pl.BlockSpec(ov.shape, lambda b, g, i: (0, 0)),
            pl.BlockSpec(expand.shape, lambda b, g, i: (0, 0, 0)),
        ],
        out_specs=pl.BlockSpec((None, tq, J * hd), lambda b, g, i: (b, i, g)),
        scratch_shapes=[
            pltpu.VMEM((J * tq, width), BF16),
            pltpu.VMEM((tq, width), F32),
            pltpu.VMEM((J * tq, LANES), F32),
            pltpu.VMEM((J * tq, LANES), F32),
            pltpu.VMEM((J * tq, LANES), F32),
            pltpu.VMEM((J * tq, hd), F32),
        ],
        compiler_params=_params("parallel", "parallel", "parallel"),
        name="nsa_attn",
    )(r3(qr), kcmp, vcmp, r3(ks), r3(vs), r3(kw), r3(vw), r3(gates), ov, expand)
    return out.reshape(batch * seq, N_HEADS * hd)


def _route_kernel(lg_ref, tri_ref, mi_ref, mw_ref, cnt_ref, carry_ref):
    @pl.when(pl.program_id(0) == 0)
    def _():
        carry_ref[...] = jnp.zeros_like(carry_ref)

    lane = lax.broadcasted_iota(I32, lg_ref.shape, 1)
    lg = jnp.where(lane < N_EXPERTS, lg_ref[...], -jnp.inf)
    m1 = jnp.max(lg, axis=-1, keepdims=True)
    i1 = jnp.min(jnp.where(lg == m1, lane, LANES), axis=-1, keepdims=True)
    lg2 = jnp.where(lane == i1, -jnp.inf, lg)
    m2 = jnp.max(lg2, axis=-1, keepdims=True)
    i2 = jnp.min(jnp.where(lg2 == m2, lane, LANES), axis=-1, keepdims=True)
    e2 = jnp.exp(m2 - m1)
    w1 = 1.0 / (1.0 + e2)
    w2 = e2 / (1.0 + e2)
    oh1, oh2 = lane == i1, lane == i2
    onehot = (oh1 | oh2).astype(F32)
    before = jnp.dot(tri_ref[...], onehot.astype(BF16), preferred_element_type=F32) + carry_ref[...]
    pos1 = jnp.sum(jnp.where(oh1, before, 0.0), axis=-1, keepdims=True).astype(I32)
    pos2 = jnp.sum(jnp.where(oh2, before, 0.0), axis=-1, keepdims=True).astype(I32)
    carry_ref[...] += jnp.sum(onehot, axis=0, keepdims=True)
    mi_ref[...] = jnp.where(lane == 0, i1, jnp.where(lane == 1, i2,
                            jnp.where(lane == 2, pos1, jnp.where(lane == 3, pos2, 0))))
    mw_ref[...] = jnp.where(lane == 0, w1, jnp.where(lane == 1, w2, 0.0))
    cnt_ref[...] = carry_ref[...]


def _route(logits):
    t = logits.shape[0]
    tm = _tile(t, 512)
    tri = (jnp.arange(tm)[None, :] < jnp.arange(tm)[:, None]).astype(BF16)
    return pl.pallas_call(
        _route_kernel,
        out_shape=(jax.ShapeDtypeStruct((t, LANES), I32), jax.ShapeDtypeStruct((t, LANES), F32),
                   jax.ShapeDtypeStruct((1, LANES), F32)),
        grid=(t // tm,),
        in_specs=[pl.BlockSpec((tm, LANES), lambda i: (i, 0)), pl.BlockSpec((tm, tm), lambda i: (0, 0))],
        out_specs=(pl.BlockSpec((tm, LANES), lambda i: (i, 0)), pl.BlockSpec((tm, LANES), lambda i: (i, 0)),
                   pl.BlockSpec((1, LANES), lambda i: (0, 0))),
        scratch_shapes=[pltpu.VMEM((1, LANES), F32)],
        compiler_params=_params("arbitrary"),
        name="moe_route",
    )(logits, tri)


def _row_copy(src_hbm, row, buf, slot, r, sem):
    return pltpu.make_async_copy(src_hbm.at[pl.ds(row, 1), :], buf.at[slot, pl.ds(r, 1), :], sem.at[slot])


def _issue_rows(idx_ref, src_hbm, buf, slot, base, sem, rows):
    def body(r, carry):
        _row_copy(src_hbm, idx_ref[0, r], buf, slot, base + r, sem).start()
        return carry

    lax.fori_loop(0, rows, body, 0, unroll=8)


def _wait_rows(src_hbm, buf, slot, sem, rows):
    def body(r, carry):
        _row_copy(src_hbm, 0, buf, slot, r, sem).wait()
        return carry

    lax.fori_loop(0, rows, body, 0, unroll=8)


def _dispatch_kernel(idx_ref, nxt_ref, src_hbm, o_ref, buf, sem, *, rows):
    i = pl.program_id(0)
    slot = i % 2

    @pl.when(i == 0)
    def _():
        _issue_rows(idx_ref, src_hbm, buf, 0, 0, sem, rows)

    @pl.when(i + 1 < pl.num_programs(0))
    def _():
        _issue_rows(nxt_ref, src_hbm, buf, 1 - slot, 0, sem, rows)

    _wait_rows(src_hbm, buf, slot, sem, rows)
    o_ref[...] = buf[slot].astype(o_ref.dtype)


def _dispatch_rows(src, idx):
    n = idx.shape[0]
    d = src.shape[1]
    tg = _tile(n, TG)
    steps = n // tg
    smem = lambda f: pl.BlockSpec((None, 1, tg), f, memory_space=pltpu.SMEM)
    idx3 = idx.reshape(steps, 1, tg)
    return pl.pallas_call(
        functools.partial(_dispatch_kernel, rows=tg),
        out_shape=jax.ShapeDtypeStruct((n, d), BF16),
        grid=(steps,),
        in_specs=[
            smem(lambda i: (i, 0, 0)),
            smem(lambda i: (jnp.minimum(i + 1, steps - 1), 0, 0)),
            pl.BlockSpec(memory_space=pl.ANY),
        ],
        out_specs=pl.BlockSpec((tg, d), lambda i: (i, 0)),
        scratch_shapes=[pltpu.VMEM((2, tg, d), F32), pltpu.SemaphoreType.DMA((2,))],
        compiler_params=_params("arbitrary"),
        name="moe_dispatch",
    )(idx3, idx3, src)


def _combine_kernel(a_ref, b_ref, na_ref, nb_ref, y_hbm, x_ref, g_ref, mw_ref, o_ref, buf, sem, *, rows):
    i = pl.program_id(0)
    slot = i % 2

    def issue(first_ref, second_ref, s):
        _issue_rows(first_ref, y_hbm, buf, s, 0, sem, rows)
        _issue_rows(second_ref, y_hbm, buf, s, rows, sem, rows)

    @pl.when(i == 0)
    def _():
        issue(a_ref, b_ref, 0)

    @pl.when(i + 1 < pl.num_programs(0))
    def _():
        issue(na_ref, nb_ref, 1 - slot)

    _wait_rows(y_hbm, buf, slot, sem, 2 * rows)
    mw = mw_ref[...]
    f = mw[:, 0:1] * buf[slot, 0:rows, :] + mw[:, 1:2] * buf[slot, rows:2 * rows, :]
    o_ref[...] = x_ref[...] + g_ref[...] * f


def _combine_rows(ys, d1, d2, meta_w, x, gate, seq):
    t, d = x.shape
    b = gate.shape[0]
    tc = _tile(seq, TCMB)
    steps = t // tc
    smem = lambda f: pl.BlockSpec((None, 1, tc), f, memory_space=pltpu.SMEM)
    cur, nxt = (lambda i: (i, 0, 0)), (lambda i: (jnp.minimum(i + 1, steps - 1), 0, 0))
    a3, b3 = d1.reshape(steps, 1, tc), d2.reshape(steps, 1, tc)
    return pl.pallas_call(
        functools.partial(_combine_kernel, rows=tc),
        out_shape=jax.ShapeDtypeStruct((t, d), F32),
        grid=(steps,),
        in_specs=[
            smem(cur), smem(cur), smem(nxt), smem(nxt),
            pl.BlockSpec(memory_space=pl.ANY),
            pl.BlockSpec((tc, d), lambda i: (i, 0)),
            pl.BlockSpec((None, 1, d), lambda i: (i * tc // seq, 0, 0)),
            pl.BlockSpec((tc, LANES), lambda i: (i, 0)),
        ],
        out_specs=pl.BlockSpec((tc, d), lambda i: (i, 0)),
        scratch_shapes=[pltpu.VMEM((2, 2 * tc, d), F32), pltpu.SemaphoreType.DMA((2,))],
        compiler_params=_params("arbitrary"),
        name="moe_combine",
    )(a3, b3, a3, b3, ys, x, gate.reshape(b, 1, d), meta_w)


def _moe_ffn_residual(x, hf, logits, gate, w_gate, w_up, w_down, seq):
    t, d = hf.shape
    n_exp = w_gate.shape[0]
    tm = _tile(t, TM_MOE)
    meta_i, meta_w, counts = _route(logits)
    e1, e2, p1, p2 = meta_i[:, 0], meta_i[:, 1], meta_i[:, 2], meta_i[:, 3]
    cnt = counts[0, :n_exp].astype(I32)
    padded = (cnt + tm - 1) // tm * tm
    ends = jnp.cumsum(padded)
    starts = ends - padded
    d1, d2 = starts[e1] + p1, starts[e2] + p2
    n_rows = 2 * t + n_exp * tm
    tok = jnp.arange(t, dtype=I32)
    src = jnp.zeros((n_rows,), I32).at[jnp.concatenate([d1, d2])].set(jnp.concatenate([tok, tok]))
    tiles = jnp.arange(n_rows // tm, dtype=I32)
    tile_expert = jnp.minimum(jnp.sum((tiles * tm)[:, None] >= ends[None, :], axis=1), n_exp - 1).astype(I32)
    tile_row = jnp.minimum(tiles, ends[-1] // tm - 1)
    tile_first = jnp.concatenate([jnp.ones((1,), I32), (tile_expert[1:] != tile_expert[:-1]).astype(I32)])
    sched = (tile_expert, tile_row, tile_first)

    xs = _dispatch_rows(hf, src)
    hs = _ws_mm(_ws_glu_kernel, xs, (w_gate, w_up), sched, tm, TN_MOE, BF16, "moe_glu")
    ys = _ws_mm(_ws_mm_kernel, hs, (w_down,), sched, tm, TN_MOE_DOWN, F32, "moe_down")
    return _combine_rows(ys, d1, d2, meta_w, x, gate, seq)


def kernel(x, c, ada_w, ada_b, norm_mix_g, norm_ffn_g, conv_w_in, conv_w, conv_w_out, nsa_w_in, nsa_w_o,
           nsa_q_norm_g, nsa_k_norm_g, cmp_pos, cmp_w1, cmp_w2, ffn_w_gate, ffn_w_up, ffn_w_down,
           router_w, router_b, moe_w_gate, moe_w_up, moe_w_down):
    batch, seq, d = x.shape
    t = batch * seq
    bf = lambda a: a.astype(BF16)

    mod = _ada_mod(c, ada_w, ada_b)
    sh_m, sc_m, g_m, sh_f, sc_f, g_f = [mod[:, :, i * d:(i + 1) * d] for i in range(6)]
    xt = x.reshape(t, d)

    hm = _norm_mod(xt, norm_mix_g[0], sc_m[0], sh_m[0], seq)
    z = _conv_mix(hm, bf(conv_w_in[0]), conv_w[0], batch, seq)
    xt = _mm_res(z, bf(conv_w_out[0]), xt, g_m[0], seq, "conv_out")
    hf = _norm_mod(xt, norm_ffn_g[0], sc_f[0], sh_f[0], seq)
    tm_ffn = _tile(t, 2 * TM)
    hid = _ws_mm(_ws_glu_kernel, hf, (ffn_w_gate[0:1], ffn_w_up[0:1]), _dense_sched(t // tm_ffn), tm_ffn,
                 TN_GLU, BF16, "ffn_glu")
    xt = _mm_res(hid, bf(ffn_w_down[0]), xt, g_f[0], seq, "ffn_down", tm=TM_DOWN, tn=TN_DOWN)

    hm = _norm_mod(xt, norm_mix_g[1], sc_m[1], sh_m[1], seq)
    qkv_w = N_HEADS * HEAD_DIM + 6 * KV_WIDTH
    w_in = bf(nsa_w_in[0])
    proj = _mm(hm, w_in, F32, "nsa_in", n=qkv_w)
    w_gates = jnp.pad(w_in[:, qkv_w:], ((0, 0), (0, LANES - 3 * N_HEADS)))
    gates_pre = _mm(hm, w_gates, F32, "nsa_gates")
    pos = jnp.arange(seq, dtype=I32)
    cos, sin = _rope_tables(pos)
    qr, ks, vs, kw, vw, gates = _nsa_prep(proj, gates_pre, cos, sin, nsa_q_norm_g[0], nsa_k_norm_g[0], seq)
    cmp_end = jnp.arange(seq // CMP_STRIDE, dtype=I32) * CMP_STRIDE + CMP_BLOCK - 1
    cos_c, sin_c = _rope_tables(cmp_end)
    kcmp, vcmp = _compress(proj.reshape(batch, seq, qkv_w), bf(cmp_w1[0]), bf(cmp_w2[0]), cmp_pos[0],
                           nsa_k_norm_g[0, 0:1], cos_c, sin_c)
    attn = _nsa_attn(qr, kcmp, vcmp, ks, vs, kw, vw, gates, batch, seq)
    xt = _mm_res(attn, bf(nsa_w_o[0]), xt, g_m[1], seq, "nsa_out")

    n_exp = router_w.shape[2]
    rw = jnp.pad(router_w[0], ((0, 0), (0, LANES - n_exp)))
    rb = jnp.pad(router_b[0], (0, LANES - n_exp)).reshape(1, LANES)
    hf, logits = _norm_mod(xt, norm_ffn_g[1], sc_f[1], sh_f[1], seq, router=(rw, rb))
    out = _moe_ffn_residual(xt, hf, logits, g_f[1], moe_w_gate[0], moe_w_up[0], moe_w_down[0], seq)
    return out.reshape(batch, seq, d)
```

```python
import functools

import jax
import jax.numpy as jnp
from jax import lax
from jax.experimental import pallas as pl
from jax.experimental.pallas import tpu as pltpu

F32 = jnp.float32
BF16 = jnp.bfloat16
I32 = jnp.int32

EPS = 1e-6
NEG_INF = -1e30
FORCE_BONUS = 1e4
ROPE_THETA = 10000.0
LOG2E = 1.4426950408889634

CONV_WIDTH = 3
N_HEADS = 32
HEAD_DIM = 128
N_KV_GROUPS = 4
HEADS_PER_GROUP = N_HEADS // N_KV_GROUPS
KV_WIDTH = N_KV_GROUPS * HEAD_DIM
CMP_BLOCK = 32
CMP_STRIDE = 16
SEL_BLOCK = 64
SEL_TOP_N = 16
WINDOW = 512
N_EXPERTS = 8

LANES = 128
VMEM_LIMIT_V7X = 60 * 1024 * 1024

TM = 1024
TN = 1024
TN_RES = 1024
TN_GLU = 256
TM_DOWN = 512
TN_DOWN = 512
TM_ROW = 512
TC_CONV = 512
TQ = 256
TK_SEL = 512
ATTN_ROW_BLOCK = 64
TM_MOE = 512
TN_MOE = 512
TN_MOE_DOWN = 1024
TG = 512
TCMB = 256


def _params(*sem):
    return pltpu.CompilerParams(dimension_semantics=sem, vmem_limit_bytes=VMEM_LIMIT_V7X)


def _tile(full, want):
    return want if full % want == 0 else full


def _ada_kernel(c_ref, w_ref, b_ref, o_ref):
    c = c_ref[...]
    cond = c * jax.nn.sigmoid(c)
    acc = jnp.dot(cond.astype(BF16), w_ref[...].astype(BF16), preferred_element_type=F32)
    o_ref[...] = acc + b_ref[...]


def _ada_mod(c, ada_w, ada_b):
    depth, d, n = ada_w.shape
    b = c.shape[0]
    tn = _tile(n, 512)
    return pl.pallas_call(
        _ada_kernel,
        out_shape=jax.ShapeDtypeStruct((depth, b, n), F32),
        grid=(depth, n // tn),
        in_specs=[
            pl.BlockSpec((b, d), lambda l, j: (0, 0)),
            pl.BlockSpec((None, d, tn), lambda l, j: (l, 0, j)),
            pl.BlockSpec((None, 1, tn), lambda l, j: (l, 0, j)),
        ],
        out_specs=pl.BlockSpec((None, b, tn), lambda l, j: (l, 0, j)),
        compiler_params=_params("parallel", "parallel"),
        name="ada_mod",
    )(c, ada_w, ada_b.reshape(depth, 1, n))


def _norm_mod_value(x_ref, g_ref, sc_ref, sh_ref):
    x = x_ref[...]
    ms = jnp.mean(x * x, axis=-1, keepdims=True)
    y = x * lax.rsqrt(ms + EPS) * g_ref[...]
    return y * (1.0 + sc_ref[...]) + sh_ref[...]


def _norm_mod_kernel(x_ref, g_ref, sc_ref, sh_ref, o_ref):
    o_ref[...] = _norm_mod_value(x_ref, g_ref, sc_ref, sh_ref).astype(o_ref.dtype)


def _norm_mod_router_kernel(x_ref, g_ref, sc_ref, sh_ref, rw_ref, rb_ref, o_ref, lg_ref):
    h = _norm_mod_value(x_ref, g_ref, sc_ref, sh_ref)
    o_ref[...] = h.astype(o_ref.dtype)
    lg_ref[...] = jnp.dot(h, rw_ref[...], preferred_element_type=F32,
                          precision=lax.Precision.HIGHEST) + rb_ref[...]


def _norm_mod(x, g, sc, sh, seq, router=None):
    t, d = x.shape
    b = sc.shape[0]
    tm = _tile(seq, TM_ROW)
    row_specs = [
        pl.BlockSpec((tm, d), lambda i: (i, 0)),
        pl.BlockSpec((1, d), lambda i: (0, 0)),
        pl.BlockSpec((None, 1, d), lambda i: (i * tm // seq, 0, 0)),
        pl.BlockSpec((None, 1, d), lambda i: (i * tm // seq, 0, 0)),
    ]
    args = [x, g.reshape(1, d), sc.reshape(b, 1, d), sh.reshape(b, 1, d)]
    if router is None:
        return pl.pallas_call(
            _norm_mod_kernel,
            out_shape=jax.ShapeDtypeStruct((t, d), BF16),
            grid=(t // tm,),
            in_specs=row_specs,
            out_specs=pl.BlockSpec((tm, d), lambda i: (i, 0)),
            compiler_params=_params("parallel"),
            name="norm_mod",
        )(*args)
    rw, rb = router
    return pl.pallas_call(
        _norm_mod_router_kernel,
        out_shape=(jax.ShapeDtypeStruct((t, d), F32), jax.ShapeDtypeStruct((t, LANES), F32)),
        grid=(t // tm,),
        in_specs=row_specs + [
            pl.BlockSpec((d, LANES), lambda i: (0, 0)),
            pl.BlockSpec((1, LANES), lambda i: (0, 0)),
        ],
        out_specs=(pl.BlockSpec((tm, d), lambda i: (i, 0)), pl.BlockSpec((tm, LANES), lambda i: (i, 0))),
        compiler_params=_params("parallel"),
        name="norm_mod_router",
    )(*args, rw, rb)


def _mm_kernel(x_ref, w_ref, o_ref):
    o_ref[...] = jnp.dot(x_ref[...], w_ref[...], preferred_element_type=F32).astype(o_ref.dtype)


def _mm(x, w, out_dtype, name, tm=TM, tn=TN, n=None):
    m, k = x.shape
    n = w.shape[1] if n is None else n
    tm, tn = _tile(m, tm), _tile(n, tn)
    return pl.pallas_call(
        _mm_kernel,
        out_shape=jax.ShapeDtypeStruct((m, n), out_dtype),
        grid=(m // tm, n // tn),
        in_specs=[pl.BlockSpec((tm, k), lambda i, j: (i, 0)), pl.BlockSpec((k, tn), lambda i, j: (0, j))],
        out_specs=pl.BlockSpec((tm, tn), lambda i, j: (i, j)),
        compiler_params=_params("parallel", "parallel"),
        name=name,
    )(x, w)


def _mm_res_kernel(x_ref, w_ref, res_ref, g_ref, o_ref):
    o_ref[...] = res_ref[...] + g_ref[...] * jnp.dot(x_ref[...], w_ref[...], preferred_element_type=F32)


def _mm_res(x, w, res, gate, seq, name, tm=TM, tn=TN_RES):
    m, k = x.shape
    n = w.shape[1]
    b = gate.shape[0]
    tm, tn = _tile(seq, tm), _tile(n, tn)
    return pl.pallas_call(
        _mm_res_kernel,
        out_shape=jax.ShapeDtypeStruct((m, n), F32),
        grid=(m // tm, n // tn),
        in_specs=[
            pl.BlockSpec((tm, k), lambda i, j: (i, 0)),
            pl.BlockSpec((k, tn), lambda i, j: (0, j)),
            pl.BlockSpec((tm, tn), lambda i, j: (i, j)),
            pl.BlockSpec((None, 1, tn), lambda i, j: (i * tm // seq, 0, j)),
        ],
        out_specs=pl.BlockSpec((tm, tn), lambda i, j: (i, j)),
        compiler_params=_params("parallel", "parallel"),
        name=name,
    )(x, w, res, gate.reshape(b, 1, n))


def _silu_mul(a, b):
    return (a * jax.nn.sigmoid(a)) * b


def _ws_glu_kernel(te_ref, tr_ref, tf_ref, x_ref, wg_ref, wu_ref, o_ref, wg_bf, wu_bf):
    i = pl.program_id(1)

    @pl.when(tf_ref[i] == 1)
    def _():
        wg_bf[...] = wg_ref[...].astype(BF16)
        wu_bf[...] = wu_ref[...].astype(BF16)

    @pl.when(tr_ref[i] == i)
    def _():
        x = x_ref[...]
        a = jnp.dot(x, wg_bf[...], preferred_element_type=F32)
        b = jnp.dot(x, wu_bf[...], preferred_element_type=F32)
        o_ref[...] = _silu_mul(a, b).astype(o_ref.dtype)

    @pl.when(tr_ref[i] != i)
    def _():
        o_ref[...] = jnp.zeros_like(o_ref)


def _ws_mm_kernel(te_ref, tr_ref, tf_ref, x_ref, w_ref, o_ref, w_bf):
    i = pl.program_id(1)

    @pl.when(tf_ref[i] == 1)
    def _():
        w_bf[...] = w_ref[...].astype(BF16)

    @pl.when(tr_ref[i] == i)
    def _():
        o_ref[...] = jnp.dot(x_ref[...], w_bf[...], preferred_element_type=F32).astype(o_ref.dtype)

    @pl.when(tr_ref[i] != i)
    def _():
        o_ref[...] = jnp.zeros_like(o_ref)


def _ws_mm(kernel, xs, ws, sched, tm, tn, out_dtype, name):
    r, k = xs.shape
    n = ws[0].shape[2]
    tn = _tile(n, tn)
    w_spec = pl.BlockSpec((None, k, tn), lambda j, i, te, tr, tf: (te[i], 0, j))
    return pl.pallas_call(
        kernel,
        out_shape=jax.ShapeDtypeStruct((r, n), out_dtype),
        grid_spec=pltpu.PrefetchScalarGridSpec(
            num_scalar_prefetch=3,
            grid=(n // tn, r // tm),
            in_specs=[pl.BlockSpec((tm, k), lambda j, i, te, tr, tf: (tr[i], 0))] + [w_spec] * len(ws),
            out_specs=pl.BlockSpec((tm, tn), lambda j, i, te, tr, tf: (i, j)),
            scratch_shapes=[pltpu.VMEM((k, tn), BF16)] * len(ws),
        ),
        compiler_params=_params("parallel", "arbitrary"),
        name=name,
    )(*sched, xs, *ws)


def _dense_sched(n_tiles):
    tiles = jnp.arange(n_tiles, dtype=I32)
    return jnp.zeros((n_tiles,), I32), tiles, (tiles == 0).astype(I32)


SUBLANES = 8


def _conv_mix_kernel(x_ref, wb_ref, wc_ref, wu_ref, cw_ref, o_ref, tail_ref):
    @pl.when(pl.program_id(2) == 0)
    def _():
        tail_ref[...] = jnp.zeros_like(tail_ref)

    x = x_ref[...]
    v = (jnp.dot(x, wc_ref[...], preferred_element_type=F32)
         * jnp.dot(x, wu_ref[...], preferred_element_type=F32))
    prev = tail_ref[...]
    tail_ref[...] = v[v.shape[0] - SUBLANES:, :]
    rows = lax.broadcasted_iota(I32, v.shape, 0)
    rest = v[SUBLANES:, :]
    v1 = jnp.where(rows >= 1, pltpu.roll(v, 1, 0), jnp.concatenate([pltpu.roll(prev, 1, 0), rest], axis=0))
    v2 = jnp.where(rows >= 2, pltpu.roll(v, 2, 0), jnp.concatenate([pltpu.roll(prev, 2, 0), rest], axis=0))
    w = cw_ref[...]
    y = w[0:1, :] * v2 + w[1:2, :] * v1 + w[2:3, :] * v
    o_ref[...] = (jnp.dot(x, wb_ref[...], preferred_element_type=F32) * y).astype(o_ref.dtype)


def _conv_mix(h, w_in, conv_w, batch, seq):
    t, k = h.shape
    d = conv_w.shape[1]
    tm = _tile(seq, TM)
    tc = _tile(d, TC_CONV)
    nc, nh = d // tc, seq // tm
    w_spec = lambda part: pl.BlockSpec((k, tc), lambda b, j, r: (0, part * nc + j))
    return pl.pallas_call(
        _conv_mix_kernel,
        out_shape=jax.ShapeDtypeStruct((t, d), BF16),
        grid=(batch, nc, nh),
        in_specs=[
            pl.BlockSpec((tm, k), lambda b, j, r: (b * nh + r, 0)),
            w_spec(0), w_spec(1), w_spec(2),
            pl.BlockSpec((CONV_WIDTH, tc), lambda b, j, r: (0, j)),
        ],
        out_specs=pl.BlockSpec((tm, tc), lambda b, j, r: (b * nh + r, j)),
        scratch_shapes=[pltpu.VMEM((SUBLANES, tc), F32)],
        compiler_params=_params("parallel", "parallel", "arbitrary"),
        name="conv_mix",
    )(h, w_in, w_in, w_in, conv_w)


def _rope_tables(pos):
    half = HEAD_DIM // 2
    inv = ROPE_THETA ** (-jnp.arange(half, dtype=F32) / half)
    ang = pos.astype(F32)[:, None] * inv[None, :]
    cos, sin = jnp.cos(ang), jnp.sin(ang)
    return jnp.concatenate([cos, cos], axis=-1), jnp.concatenate([-sin, sin], axis=-1)


def _norm_rope(x, g, cos, sin_signed, ones=None):
    if ones is None:
        ms = jnp.mean(x * x, axis=-1, keepdims=True)
    else:
        ms = jnp.dot((x * x).astype(BF16), ones, preferred_element_type=F32) * (1.0 / HEAD_DIM)
    xn = x * lax.rsqrt(ms + EPS) * g
    return xn * cos + pltpu.roll(xn, HEAD_DIM // 2, 1) * sin_signed


def _nsa_prep_kernel(p_ref, gp_ref, cos_ref, sin_ref, qg_ref, kg_ref,
                     q_o, ks_o, vs_o, kw_o, vw_o, g_o, *, scale):
    cos, sin = cos_ref[...], sin_ref[...]
    hd = HEAD_DIM
    qg = qg_ref[...]
    ones = jnp.ones((hd, hd), BF16)
    for h in range(N_HEADS):
        x = p_ref[:, h * hd:(h + 1) * hd]
        q_o[:, h * hd:(h + 1) * hd] = (_norm_rope(x, qg, cos, sin, ones) * scale).astype(q_o.dtype)
    kv0 = N_HEADS * hd
    for g in range(N_KV_GROUPS):
        def col(i, g=g):
            c0 = kv0 + i * KV_WIDTH + g * hd
            return p_ref[:, c0:c0 + hd]
        sl = slice(g * hd, (g + 1) * hd)
        ks_o[:, sl] = _norm_rope(col(2), kg_ref[1:2, :], cos, sin, ones).astype(ks_o.dtype)
        vs_o[:, sl] = col(3).astype(vs_o.dtype)
        kw_o[:, sl] = _norm_rope(col(4), kg_ref[2:3, :], cos, sin, ones).astype(kw_o.dtype)
        vw_o[:, sl] = col(5).astype(vw_o.dtype)
    sig = jax.nn.sigmoid(gp_ref[...])
    per_group = 3 * HEADS_PER_GROUP
    for g in range(N_KV_GROUPS):
        shift = (LANES - g * per_group) % LANES
        g_o[:, g * LANES:(g + 1) * LANES] = sig if shift == 0 else pltpu.roll(sig, shift, 1)


def _nsa_prep(proj, gates_pre, cos, sin, q_g, k_g, seq):
    t = proj.shape[0]
    tm = _tile(seq, 256)
    nseq = seq // tm
    qd = N_HEADS * HEAD_DIM
    kvspec = pl.BlockSpec((tm, KV_WIDTH), lambda i: (i, 0))
    kvshape = jax.ShapeDtypeStruct((t, KV_WIDTH), BF16)
    return pl.pallas_call(
        functools.partial(_nsa_prep_kernel, scale=HEAD_DIM ** -0.5 * LOG2E),
        out_shape=(jax.ShapeDtypeStruct((t, qd), BF16), kvshape, kvshape, kvshape, kvshape,
                   jax.ShapeDtypeStruct((t, N_KV_GROUPS * LANES), F32)),
        grid=(t // tm,),
        in_specs=[
            pl.BlockSpec((tm, proj.shape[1]), lambda i: (i, 0)),
            pl.BlockSpec((tm, LANES), lambda i: (i, 0)),
            pl.BlockSpec((tm, HEAD_DIM), lambda i: (i % nseq, 0)),
            pl.BlockSpec((tm, HEAD_DIM), lambda i: (i % nseq, 0)),
            pl.BlockSpec((1, HEAD_DIM), lambda i: (0, 0)),
            pl.BlockSpec((3, HEAD_DIM), lambda i: (0, 0)),
        ],
        out_specs=(pl.BlockSpec((tm, qd), lambda i: (i, 0)), kvspec, kvspec, kvspec, kvspec,
                   pl.BlockSpec((tm, N_KV_GROUPS * LANES), lambda i: (i, 0))),
        compiler_params=_params("parallel"),
        name="nsa_prep",
    )(proj, gates_pre, cos, sin, q_g.reshape(1, HEAD_DIM), k_g)


def _compress_kernel(kc_ref, vc_ref, w1_ref, w2_ref, pos_ref, kg_ref, cos_ref, sin_ref,
                     ko_ref, vo_ref, *, nchunk):
    def comp(x_ref, i):
        first = jnp.zeros((nchunk, w1_ref.shape[-1]), F32)
        second = jnp.zeros_like(first)
        for l in range(CMP_STRIDE):
            xl = x_ref[pl.ds(l, nchunk, stride=CMP_STRIDE), :]
            xa = (xl + pos_ref[i, l:l + 1, :]).astype(BF16)
            xb = (xl + pos_ref[i, CMP_STRIDE + l:CMP_STRIDE + l + 1, :]).astype(BF16)
            first += jnp.dot(xa, w1_ref[i, l], preferred_element_type=F32)
            second += jnp.dot(xb, w1_ref[i, CMP_STRIDE + l], preferred_element_type=F32)
        pre = first + pltpu.roll(second, nchunk - 1, 0)
        hid = pre * jax.nn.sigmoid(pre)
        return jnp.dot(hid.astype(BF16), w2_ref[i], preferred_element_type=F32)

    k = comp(kc_ref, 0)
    ko_ref[...] = _norm_rope(k, kg_ref[...], cos_ref[...], sin_ref[...]).astype(ko_ref.dtype)
    vo_ref[...] = comp(vc_ref, 1).astype(vo_ref.dtype)


def _compress(proj3, w1, w2, cmp_pos, kg0, cos_c, sin_c):
    batch, seq, _ = proj3.shape
    nchunk = seq // CMP_STRIDE
    hd = HEAD_DIM
    cb = N_HEADS
    out = jax.ShapeDtypeStruct((batch, N_KV_GROUPS, nchunk, hd), BF16)
    ospec = pl.BlockSpec((None, None, nchunk, hd), lambda b, g: (b, g, 0, 0))
    full = lambda a: pl.BlockSpec(a.shape, lambda b, g: (0,) * a.ndim)
    return pl.pallas_call(
        functools.partial(_compress_kernel, nchunk=nchunk),
        out_shape=(out, out),
        grid=(batch, N_KV_GROUPS),
        in_specs=[
            pl.BlockSpec((None, seq, hd), lambda b, g: (b, 0, cb + g)),
            pl.BlockSpec((None, seq, hd), lambda b, g: (b, 0, cb + N_KV_GROUPS + g)),
            full(w1), full(w2), full(cmp_pos), full(kg0), full(cos_c), full(sin_c),
        ],
        out_specs=(ospec, ospec),
        compiler_params=_params("parallel", "parallel"),
        name="nsa_compress",
    )(proj3, proj3, w1, w2, cmp_pos, kg0, cos_c, sin_c)


def _nsa_attn_kernel(q_ref, kc_ref, vc_ref, ks_ref, vs_ref, kw_ref, vw_ref, g_ref, ov_ref, e_ref,
                     o_ref, p_ref, b_ref, m_ref, l_ref, a_ref, acc_ref,
                     *, tq, tk, seq, n_sel, n_top, span):
    J = HEADS_PER_GROUP
    hd = HEAD_DIM
    q0 = pl.program_id(2) * tq
    q = q_ref[...]
    qa = jnp.concatenate([q[:, j * hd:(j + 1) * hd] for j in range(J)], axis=0)
    tpos = q0 + lax.broadcasted_iota(I32, (tq, 1), 0)

    def scores(kblk):
        return lax.dot_general(qa, kblk, (((1,), (1,)), ((), ())), preferred_element_type=F32)

    ncp = kc_ref.shape[0]
    n_idx = lax.broadcasted_iota(I32, (1, ncp), 1)
    mask_c = (n_idx * CMP_STRIDE + (CMP_BLOCK - 1)) <= tpos
    s3 = jnp.where(mask_c[None], scores(kc_ref[...]).reshape(J, tq, ncp), NEG_INF)
    m = jnp.max(s3, axis=-1, keepdims=True)
    p = jnp.where(mask_c[None], jnp.exp2(s3 - m), 0.0)
    l = jnp.sum(p, axis=-1, keepdims=True)
    pc = p * jnp.where(l > 0.0, 1.0 / l, 0.0)
    o_c = jnp.dot(pc.reshape(J * tq, ncp).astype(BF16), vc_ref[...], preferred_element_type=F32)

    nsp = e_ref.shape[1]
    imp = lax.dot_general(ov_ref[...], jnp.sum(pc, axis=0), (((1,), (1,)), ((), ())),
                          preferred_element_type=F32, precision=lax.Precision.HIGHEST)[:nsp]
    blk = lax.broadcasted_iota(I32, (nsp, 1), 0)
    tpos_l = q0 + lax.broadcasted_iota(I32, (1, tq), 1)
    cur = tpos_l // SEL_BLOCK
    forced = (blk == 0) | (blk == cur) | (blk == cur - 1)
    imp = jnp.where(blk * SEL_BLOCK <= tpos_l, imp + FORCE_BONUS * forced.astype(F32), NEG_INF)
    rank = jnp.zeros((nsp, tq), F32)
    for mp in range(n_sel):
        row = imp[mp:mp + 1, :]
        rank += jnp.where(blk > mp, (row >= imp).astype(F32), (row > imp).astype(F32))
    sel = jnp.transpose(((rank < n_top) & (blk < n_sel)).astype(F32)).astype(BF16)

    rb = ATTN_ROW_BLOCK

    def softmax_head(j, kblk, width, first, keep_max):
        s_head = lax.dot_general(q[:, j * hd:(j + 1) * hd], kblk, (((1,), (1,)), ((), ())),
                                 preferred_element_type=F32)
        lane_tiles = [slice(c, c + LANES) for c in range(0, width, LANES)]
        for h0 in range(0, tq, rb):
            rows = slice(j * tq + h0, j * tq + h0 + rb)
            s = [s_head[h0:h0 + rb, c] + b_ref[h0:h0 + rb, c] for c in lane_tiles]
            m_new = jnp.broadcast_to(jnp.max(functools.reduce(jnp.maximum, s), axis=-1, keepdims=True),
                                     (rb, LANES))
            if not first:
                m_old = m_ref[rows, :]
                m_new = jnp.maximum(m_old, m_new)
                alpha = jnp.exp2(m_old - m_new)
                a_ref[rows, :] = alpha
            if keep_max:
                m_ref[rows, :] = m_new
            p = [jnp.exp2(x - m_new) for x in s]
            row_sum = jnp.broadcast_to(jnp.sum(functools.reduce(jnp.add, p), axis=-1, keepdims=True),
                                       (rb, LANES))
            l_ref[rows, :] = row_sum if first else alpha * l_ref[rows, :] + row_sum
            for c, x in zip(lane_tiles, p):
                p_ref[rows, c] = x.astype(BF16)

    def sel_chunk(c, first):
        k0 = c * tk if first else pl.multiple_of(c * tk, tk)
        picked = jnp.dot(sel, e_ref[c], preferred_element_type=F32)
        kpos = k0 + lax.broadcasted_iota(I32, (1, tk), 1)
        b_ref[:, :tk] = jnp.where((picked > 0.5) & (kpos <= tpos), 0.0, NEG_INF)
        kblk = ks_ref[pl.ds(k0, tk), :]
        for j in range(J):
            softmax_head(j, kblk, tk, first, keep_max=True)
        pv = jnp.dot(p_ref[:, :tk], vs_ref[pl.ds(k0, tk), :], preferred_element_type=F32)
        acc_ref[...] = pv if first else a_ref[...] * acc_ref[...] + pv

    sel_chunk(0, True)

    def later_chunk(c, carry):
        sel_chunk(c, False)
        return carry

    lax.fori_loop(1, (q0 + tq + tk - 1) // tk, later_chunk, 0)
    inv_ls = 1.0 / l_ref[...]

    start = pl.multiple_of(jnp.maximum(q0 + tq - span, 0), tq)
    diff = tpos - (start + lax.broadcasted_iota(I32, (1, span), 1))
    b_ref[:, :span] = jnp.where((diff >= 0) & (diff < WINDOW), 0.0, NEG_INF)
    kblk = kw_ref[pl.ds(start, span), :]
    for j in range(J):
        softmax_head(j, kblk, span, True, keep_max=False)
    o_w = jnp.dot(p_ref[:, :span], vw_ref[pl.ds(start, span), :], preferred_element_type=F32)
    inv_lw = 1.0 / l_ref[...]

    gt = g_ref[...]
    for j in range(J):
        rows = slice(j * tq, (j + 1) * tq)
        o = (gt[:, 3 * j:3 * j + 1] * o_c[rows]
             + (gt[:, 3 * j + 1:3 * j + 2] * inv_ls[rows]) * acc_ref[rows, :]
             + (gt[:, 3 * j + 2:3 * j + 3] * inv_lw[rows]) * o_w[rows])
        o_ref[:, j * hd:(j + 1) * hd] = o.astype(o_ref.dtype)


def _nsa_attn(qr, kcmp, vcmp, ks, vs, kw, vw, gates, batch, seq):
    hd = HEAD_DIM
    J = HEADS_PER_GROUP
    tq = _tile(seq, TQ)
    tk = _tile(seq, TK_SEL)
    n_sel = seq // SEL_BLOCK
    n_top = min(SEL_TOP_N, n_sel)
    span = min(WINDOW + tq, seq)
    width = max(span, tk)
    ncp = kcmp.shape[2]
    nsp = -(-n_sel // 16) * 16
    ci = jnp.arange(ncp, dtype=I32)[None, :] * CMP_STRIDE
    sj = jnp.arange(LANES, dtype=I32)[:, None] * SEL_BLOCK
    ov = jnp.clip(jnp.minimum(ci + CMP_BLOCK, sj + SEL_BLOCK) - jnp.maximum(ci, sj), 0).astype(F32) / CMP_BLOCK
    ov = jnp.where((jnp.arange(ncp)[None, :] < ncp - 1) & (jnp.arange(LANES)[:, None] < n_sel), ov, 0.0)
    kk = jnp.arange(seq, dtype=I32).reshape(seq // tk, 1, tk)
    expand = (kk // SEL_BLOCK == jnp.arange(nsp, dtype=I32)[None, :, None]).astype(BF16)

    r3 = lambda a: a.reshape(batch, seq, a.shape[-1])
    kv_spec = pl.BlockSpec((None, seq, hd), lambda b, g, i: (b, 0, g))
    cmp_spec = pl.BlockSpec((None, None, ncp, hd), lambda b, g, i: (b, g, 0, 0))
    out = pl.pallas_call(
        functools.partial(_nsa_attn_kernel, tq=tq, tk=tk, seq=seq, n_sel=n_sel, n_top=n_top, span=span),
        out_shape=jax.ShapeDtypeStruct((batch, seq, N_HEADS * hd), BF16),
        grid=(batch, N_KV_GROUPS, seq // tq),
        in_specs=[
            pl.BlockSpec((None, tq, J * hd), lambda b, g, i: (b, i, g)),
            cmp_spec, cmp_spec, kv_spec, kv_spec, kv_spec, kv_spec,
            pl.BlockSpec((None, tq, LANES), lambda b, g, i: (b, i, g)),
            pl.BlockSpec(ov.shape, lambda b, g, i: (0, 0)),
            pl.BlockSpec(expand.shape, lambda b, g, i: (0, 0, 0)),
        ],
        out_specs=pl.BlockSpec((None, tq, J * hd), lambda b, g, i: (b, i, g)),
        scratch_shapes=[
            pltpu.VMEM((J * tq, width), BF16),
            pltpu.VMEM((tq, width), F32),
            pltpu.VMEM((J * tq, LANES), F32),
            pltpu.VMEM((J * tq, LANES), F32),
            pltpu.VMEM((J * tq, LANES), F32),
            pltpu.VMEM((J * tq, hd), F32),
        ],
        compiler_params=_params("parallel", "parallel", "parallel"),
        name="nsa_attn",
    )(r3(qr), kcmp, vcmp, r3(ks), r3(vs), r3(kw), r3(vw), r3(gates), ov, expand)
    return out.reshape(batch * seq, N_HEADS * hd)


def _route_kernel(lg_ref, tri_ref, mi_ref, mw_ref, cnt_ref, carry_ref):
    @pl.when(pl.program_id(0) == 0)
    def _():
        carry_ref[...] = jnp.zeros_like(carry_ref)

    lane = lax.broadcasted_iota(I32, lg_ref.shape, 1)
    lg = jnp.where(lane < N_EXPERTS, lg_ref[...], -jnp.inf)
    m1 = jnp.max(lg, axis=-1, keepdims=True)
    i1 = jnp.min(jnp.where(lg == m1, lane, LANES), axis=-1, keepdims=True)
    lg2 = jnp.where(lane == i1, -jnp.inf, lg)
    m2 = jnp.max(lg2, axis=-1, keepdims=True)
    i2 = jnp.min(jnp.where(lg2 == m2, lane, LANES), axis=-1, keepdims=True)
    e2 = jnp.exp(m2 - m1)
    w1 = 1.0 / (1.0 + e2)
    w2 = e2 / (1.0 + e2)
    oh1, oh2 = lane == i1, lane == i2
    onehot = (oh1 | oh2).astype(F32)
    before = jnp.dot(tri_ref[...], onehot.astype(BF16), preferred_element_type=F32) + carry_ref[...]
    pos1 = jnp.sum(jnp.where(oh1, before, 0.0), axis=-1, keepdims=True).astype(I32)
    pos2 = jnp.sum(jnp.where(oh2, before, 0.0), axis=-1, keepdims=True).astype(I32)
    carry_ref[...] += jnp.sum(onehot, axis=0, keepdims=True)
    mi_ref[...] = jnp.where(lane == 0, i1, jnp.where(lane == 1, i2,
                            jnp.where(lane == 2, pos1, jnp.where(lane == 3, pos2, 0))))
    mw_ref[...] = jnp.where(lane == 0, w1, jnp.where(lane == 1, w2, 0.0))
    cnt_ref[...] = carry_ref[...]


def _route(logits):
    t = logits.shape[0]
    tm = _tile(t, 512)
    tri = (jnp.arange(tm)[None, :] < jnp.arange(tm)[:, None]).astype(BF16)
    return pl.pallas_call(
        _route_kernel,
        out_shape=(jax.ShapeDtypeStruct((t, LANES), I32), jax.ShapeDtypeStruct((t, LANES), F32),
                   jax.ShapeDtypeStruct((1, LANES), F32)),
        grid=(t // tm,),
        in_specs=[pl.BlockSpec((tm, LANES), lambda i: (i, 0)), pl.BlockSpec((tm, tm), lambda i: (0, 0))],
        out_specs=(pl.BlockSpec((tm, LANES), lambda i: (i, 0)), pl.BlockSpec((tm, LANES), lambda i: (i, 0)),
                   pl.BlockSpec((1, LANES), lambda i: (0, 0))),
        scratch_shapes=[pltpu.VMEM((1, LANES), F32)],
        compiler_params=_params("arbitrary"),
        name="moe_route",
    )(logits, tri)


def _row_copy(src_hbm, row, buf, slot, r, sem):
    return pltpu.make_async_copy(src_hbm.at[pl.ds(row, 1), :], buf.at[slot, pl.ds(r, 1), :], sem.at[slot])


ISSUE_GROUP = 8


def _issue_rows(idx_ref, src_hbm, buf, slot, base, sem, rows, split_priorities=False):
    def body(g, carry):
        r0 = g * ISSUE_GROUP
        for k in range(ISSUE_GROUP):
            _row_copy(src_hbm, idx_ref[0, r0 + k], buf, slot, base + r0 + k, sem).start(
                priority=k % 2 if split_priorities else 0)
        return carry

    lax.fori_loop(0, rows // ISSUE_GROUP, body, 0)


def _wait_rows(src_hbm, buf, slot, sem, rows):
    def body(r, carry):
        _row_copy(src_hbm, 0, buf, slot, r, sem).wait()
        return carry

    lax.fori_loop(0, rows, body, 0, unroll=8)


def _dispatch_kernel(idx_ref, nxt_ref, src_hbm, o_ref, buf, sem, *, rows):
    i = pl.program_id(0)
    slot = i % 2

    @pl.when(i == 0)
    def _():
        _issue_rows(idx_ref, src_hbm, buf, 0, 0, sem, rows, split_priorities=True)

    @pl.when(i + 1 < pl.num_programs(0))
    def _():
        _issue_rows(nxt_ref, src_hbm, buf, 1 - slot, 0, sem, rows, split_priorities=True)

    _wait_rows(src_hbm, buf, slot, sem, rows)
    o_ref[...] = buf[slot].astype(o_ref.dtype)


def _dispatch_rows(src, idx):
    n = idx.shape[0]
    d = src.shape[1]
    tg = _tile(n, TG)
    steps = n // tg
    smem = lambda f: pl.BlockSpec((None, 1, tg), f, memory_space=pltpu.SMEM)
    idx3 = idx.reshape(steps, 1, tg)
    return pl.pallas_call(
        functools.partial(_dispatch_kernel, rows=tg),
        out_shape=jax.ShapeDtypeStruct((n, d), BF16),
        grid=(steps,),
        in_specs=[
            smem(lambda i: (i, 0, 0)),
            smem(lambda i: (jnp.minimum(i + 1, steps - 1), 0, 0)),
            pl.BlockSpec(memory_space=pl.ANY),
        ],
        out_specs=pl.BlockSpec((tg, d), lambda i: (i, 0)),
        scratch_shapes=[pltpu.VMEM((2, tg, d), F32), pltpu.SemaphoreType.DMA((2,))],
        compiler_params=_params("arbitrary"),
        name="moe_dispatch",
    )(idx3, idx3, src)


def _combine_kernel(a_ref, b_ref, na_ref, nb_ref, y_hbm, x_ref, g_ref, mw_ref, o_ref, buf, sem, *, rows):
    i = pl.program_id(0)
    slot = i % 2

    def issue(first_ref, second_ref, s):
        _issue_rows(first_ref, y_hbm, buf, s, 0, sem, rows)
        _issue_rows(second_ref, y_hbm, buf, s, rows, sem, rows)

    @pl.when(i == 0)
    def _():
        issue(a_ref, b_ref, 0)

    @pl.when(i + 1 < pl.num_programs(0))
    def _():
        issue(na_ref, nb_ref, 1 - slot)

    _wait_rows(y_hbm, buf, slot, sem, 2 * rows)
    mw = mw_ref[...]
    f = mw[:, 0:1] * buf[slot, 0:rows, :] + mw[:, 1:2] * buf[slot, rows:2 * rows, :]
    o_ref[...] = x_ref[...] + g_ref[...] * f


def _combine_rows(ys, d1, d2, meta_w, x, gate, seq):
    t, d = x.shape
    b = gate.shape[0]
    tc = _tile(seq, TCMB)
    steps = t // tc
    smem = lambda f: pl.BlockSpec((None, 1, tc), f, memory_space=pltpu.SMEM)
    cur, nxt = (lambda i: (i, 0, 0)), (lambda i: (jnp.minimum(i + 1, steps - 1), 0, 0))
    a3, b3 = d1.reshape(steps, 1, tc), d2.reshape(steps, 1, tc)
    return pl.pallas_call(
        functools.partial(_combine_kernel, rows=tc),
        out_shape=jax.ShapeDtypeStruct((t, d), F32),
        grid=(steps,),
        in_specs=[
            smem(cur), smem(cur), smem(nxt), smem(nxt),
            pl.BlockSpec(memory_space=pl.ANY),
            pl.BlockSpec((tc, d), lambda i: (i, 0)),
            pl.BlockSpec((None, 1, d), lambda i: (i * tc // seq, 0, 0)),
            pl.BlockSpec((tc, LANES), lambda i: (i, 0)),
        ],
        out_specs=pl.BlockSpec((tc, d), lambda i: (i, 0)),
        scratch_shapes=[pltpu.VMEM((2, 2 * tc, d), F32), pltpu.SemaphoreType.DMA((2,))],
        compiler_params=_params("arbitrary"),
        name="moe_combine",
    )(a3, b3, a3, b3, ys, x, gate.reshape(b, 1, d), meta_w)


def _moe_ffn_residual(x, hf, logits, gate, w_gate, w_up, w_down, seq):
    t, d = hf.shape
    n_exp = w_gate.shape[0]
    tm = _tile(t, TM_MOE)
    meta_i, meta_w, counts = _route(logits)
    e1, e2, p1, p2 = meta_i[:, 0], meta_i[:, 1], meta_i[:, 2], meta_i[:, 3]
    cnt = counts[0, :n_exp].astype(I32)
    padded = (cnt + tm - 1) // tm * tm
    ends = jnp.cumsum(padded)
    starts = ends - padded
    d1, d2 = starts[e1] + p1, starts[e2] + p2
    n_rows = 2 * t + n_exp * tm
    tok = jnp.arange(t, dtype=I32)
    src = jnp.zeros((n_rows,), I32).at[jnp.concatenate([d1, d2])].set(jnp.concatenate([tok, tok]))
    tiles = jnp.arange(n_rows // tm, dtype=I32)
    tile_expert = jnp.minimum(jnp.sum((tiles * tm)[:, None] >= ends[None, :], axis=1), n_exp - 1).astype(I32)
    tile_row = jnp.minimum(tiles, ends[-1] // tm - 1)
    tile_first = jnp.concatenate([jnp.ones((1,), I32), (tile_expert[1:] != tile_expert[:-1]).astype(I32)])
    sched = (tile_expert, tile_row, tile_first)

    xs = _dispatch_rows(hf, src)
    hs = _ws_mm(_ws_glu_kernel, xs, (w_gate, w_up), sched, tm, TN_MOE, BF16, "moe_glu")
    ys = _ws_mm(_ws_mm_kernel, hs, (w_down,), sched, tm, TN_MOE_DOWN, F32, "moe_down")
    return _combine_rows(ys, d1, d2, meta_w, x, gate, seq)


def kernel(x, c, ada_w, ada_b, norm_mix_g, norm_ffn_g, conv_w_in, conv_w, conv_w_out, nsa_w_in, nsa_w_o,
           nsa_q_norm_g, nsa_k_norm_g, cmp_pos, cmp_w1, cmp_w2, ffn_w_gate, ffn_w_up, ffn_w_down,
           router_w, router_b, moe_w_gate, moe_w_up, moe_w_down):
    batch, seq, d = x.shape
    t = batch * seq
    bf = lambda a: a.astype(BF16)

    mod = _ada_mod(c, ada_w, ada_b)
    sh_m, sc_m, g_m, sh_f, sc_f, g_f = [mod[:, :, i * d:(i + 1) * d] for i in range(6)]
    xt = x.reshape(t, d)

    hm = _norm_mod(xt, norm_mix_g[0], sc_m[0], sh_m[0], seq)
    z = _conv_mix(hm, bf(conv_w_in[0]), conv_w[0], batch, seq)
    xt = _mm_res(z, bf(conv_w_out[0]), xt, g_m[0], seq, "conv_out")
    hf = _norm_mod(xt, norm_ffn_g[0], sc_f[0], sh_f[0], seq)
    tm_ffn = _tile(t, 2 * TM)
    hid = _ws_mm(_ws_glu_kernel, hf, (ffn_w_gate[0:1], ffn_w_up[0:1]), _dense_sched(t // tm_ffn), tm_ffn,
                 TN_GLU, BF16, "ffn_glu")
    xt = _mm_res(hid, bf(ffn_w_down[0]), xt, g_f[0], seq, "ffn_down", tm=TM_DOWN, tn=TN_DOWN)

    hm = _norm_mod(xt, norm_mix_g[1], sc_m[1], sh_m[1], seq)
    qkv_w = N_HEADS * HEAD_DIM + 6 * KV_WIDTH
    w_in = bf(nsa_w_in[0])
    proj = _mm(hm, w_in, F32, "nsa_in", n=qkv_w)
    w_gates = jnp.pad(w_in[:, qkv_w:], ((0, 0), (0, LANES - 3 * N_HEADS)))
    gates_pre = _mm(hm, w_gates, F32, "nsa_gates")
    pos = jnp.arange(seq, dtype=I32)
    cos, sin = _rope_tables(pos)
    qr, ks, vs, kw, vw, gates = _nsa_prep(proj, gates_pre, cos, sin, nsa_q_norm_g[0], nsa_k_norm_g[0], seq)
    cmp_end = jnp.arange(seq // CMP_STRIDE, dtype=I32) * CMP_STRIDE + CMP_BLOCK - 1
    cos_c, sin_c = _rope_tables(cmp_end)
    kcmp, vcmp = _compress(proj.reshape(batch, seq, qkv_w), bf(cmp_w1[0]), bf(cmp_w2[0]), cmp_pos[0],
                           nsa_k_norm_g[0, 0:1], cos_c, sin_c)
    attn = _nsa_attn(qr, kcmp, vcmp, ks, vs, kw, vw, gates, batch, seq)
    xt = _mm_res(attn, bf(nsa_w_o[0]), xt, g_m[1], seq, "nsa_out")

    n_exp = router_w.shape[2]
    rw = jnp.pad(router_w[0], ((0, 0), (0, LANES - n_exp)))
    rb = jnp.pad(router_b[0], (0, LANES - n_exp)).reshape(1, LANES)
    hf, logits = _norm_mod(xt, norm_ffn_g[1], sc_f[1], sh_f[1], seq, router=(rw, rb))
    out = _moe_ffn_residual(xt, hf, logits, g_f[1], moe_w_gate[0], moe_w_up[0], moe_w_down[0], seq)
    return out.reshape(batch, seq, d)
```
